```python
import jax, jax.numpy as jnp
from jax import lax
import numpy as np

D_MODEL = 1024
BATCH = 2
SEQ = 8192
DEPTH = 1

HEAD_DIM = 64
ATTN_Q_HEADS = 8
ATTN_KV_HEADS = 4
ATTN_GROUP = ATTN_Q_HEADS // ATTN_KV_HEADS
DILATED_PATTERNS = ((128, 1), (512, 4), (2048, 16))
ATTN_BLOCK = 128
ATTN_DIM = ATTN_Q_HEADS * HEAD_DIM
KV_DIM = ATTN_KV_HEADS * HEAD_DIM
SSM_HEADS = 16
SSM_HEAD_DIM = 64
SSM_INNER = SSM_HEADS * SSM_HEAD_DIM
SSM_GROUPS = 2
SSM_STATE = 128
SSM_CONV = 4
SSM_CHUNK = 128
BC_DIM = SSM_GROUPS * SSM_STATE
MIX_DIM = ATTN_DIM + SSM_INNER
IN_PROJ_DIM = ATTN_DIM + 2 * KV_DIM + 2 * SSM_INNER + 2 * BC_DIM + SSM_HEADS
D_FF = 2816
FFN_CONV = 3
PLE_DIM = 256
EPS = 1e-6

kernel_name = 'hybrid_dilated_attn_ssd_convffn_ple'


def rms_norm(x, g):
    xf = x.astype(jnp.float32)
    y = xf * lax.rsqrt(jnp.mean(xf * xf, axis=-1, keepdims=True) + EPS)
    return (y * g.astype(jnp.float32)).astype(x.dtype)


def causal_depthwise_conv(u, w):
    k_width, chans = w.shape
    return lax.conv_general_dilated(
        u, w[:, None, :].astype(u.dtype), window_strides=(1,), padding=[(k_width - 1, 0)],
        dimension_numbers=('NWC', 'WIO', 'NWC'), feature_group_count=chans)


def _dilated_pattern(qh, kh, vh, window, dilation):
    bsz, seq = qh.shape[:2]
    steps = window // dilation
    span = dilation * ATTN_BLOCK
    s_pad = -(-seq // span) * span
    nblk = s_pad // span

    def to_sub(t):
        t = jnp.pad(t, [(0, 0), (0, s_pad - seq)] + [(0, 0)] * (t.ndim - 2))
        t = t.reshape(bsz, s_pad // dilation, dilation, *t.shape[2:])
        t = jnp.moveaxis(t, 2, 1)
        return t.reshape(bsz, dilation, nblk, ATTN_BLOCK, *t.shape[3:])

    def from_sub(t):
        t = t.reshape(bsz, dilation, nblk * ATTN_BLOCK, *t.shape[4:])
        t = jnp.moveaxis(t, 1, 2).reshape(bsz, s_pad, *t.shape[3:])
        return t[:, :seq]

    qs, ks, vs = to_sub(qh), to_sub(kh), to_sub(vh)
    pad_blk = [(0, 0), (0, 0), (1, 0), (0, 0), (0, 0), (0, 0)]
    kk = jnp.concatenate([jnp.pad(ks, pad_blk)[:, :, :-1], ks], axis=3)
    vv = jnp.concatenate([jnp.pad(vs, pad_blk)[:, :, :-1], vs], axis=3)
    s = jnp.einsum('brnikge,brnjke->brnkgij', qs, kk)
    qi = jnp.arange(ATTN_BLOCK)[:, None]
    kj = jnp.arange(2 * ATTN_BLOCK)[None, :]
    delta = ATTN_BLOCK + qi - kj
    band = (delta >= 0) & (delta <= steps)
    valid = (jnp.arange(nblk)[:, None, None] > 0) | (kj[None] >= ATTN_BLOCK)
    mask = band[None] & valid
    s = jnp.where(mask[:, None, None], s, -jnp.inf)
    m = jnp.max(s, axis=-1)
    pexp = jnp.exp(s - m[..., None])
    l = jnp.sum(pexp, axis=-1)
    o = jnp.einsum('brnkgij,brnjke->brnikge', pexp, vv)
    m = jnp.moveaxis(m, -1, 3)
    l = jnp.moveaxis(l, -1, 3)
    return from_sub(o), from_sub(m), from_sub(l)


def dilated_attention(q, k, v, q_g, k_g):
    bsz, seq, _ = q.shape
    qh = rms_norm(q.reshape(bsz, seq, ATTN_KV_HEADS, ATTN_GROUP, HEAD_DIM), q_g).astype(jnp.float32)
    qh = qh * (HEAD_DIM ** -0.5)
    kh = rms_norm(k.reshape(bsz, seq, ATTN_KV_HEADS, HEAD_DIM), k_g).astype(jnp.float32)
    vh = v.reshape(bsz, seq, ATTN_KV_HEADS, HEAD_DIM).astype(jnp.float32)
    res = [_dilated_pattern(qh, kh, vh, w, d) for (w, d) in DILATED_PATTERNS]
    m_max = jnp.max(jnp.stack([r[1] for r in res]), axis=0)
    wts = [jnp.exp(r[1] - m_max) for r in res]
    num = sum(wi[..., None] * r[0] for wi, r in zip(wts, res))
    den = sum(wi * r[2] for wi, r in zip(wts, res))
    out = num / den[..., None]
    return out.reshape(bsz, seq, ATTN_DIM).astype(q.dtype)


def ssd_chunked(xdt, a, bm, cm):
    bsz, seq, nh, hp = xdt.shape
    nc, ln, ng, ns = seq // SSM_CHUNK, SSM_CHUNK, SSM_GROUPS, SSM_STATE
    ne = nh // ng
    xc = xdt.reshape(bsz, nc, ln, ng, ne, hp)
    ac = a.reshape(bsz, nc, ln, ng, ne)
    bc = bm.reshape(bsz, nc, ln, ng, ns)
    cc = cm.reshape(bsz, nc, ln, ng, ns)
    acum = jnp.cumsum(ac, axis=2)
    causal = jnp.tril(jnp.ones((ln, ln), dtype=bool))
    seg = acum[:, :, :, None] - acum[:, :, None]
    decay = jnp.exp(jnp.where(causal[:, :, None, None], seg, -jnp.inf))
    cb = jnp.einsum('bclgn,bcsgn->bclsg', cc, bc)
    y_diag = jnp.einsum('bclsge,bcsgep->bclgep', cb[..., None] * decay, xc)
    decay_to_end = jnp.exp(acum[:, :, -1:] - acum)
    chunk_states = jnp.einsum('bclgn,bclgep->bcgepn', bc, xc * decay_to_end[..., None])
    chunk_decay = jnp.exp(acum[:, :, -1])

    def step(h, inp):
        st, dec = inp
        return h * dec[..., None, None] + st, h

    h0 = jnp.zeros((bsz, ng, ne, hp, ns), jnp.float32)
    _, h_in = lax.scan(step, h0, (jnp.moveaxis(chunk_states, 1, 0), jnp.moveaxis(chunk_decay, 1, 0)))
    h_in = jnp.moveaxis(h_in, 0, 1)
    y_off = jnp.einsum('bclgn,bcgepn->bclgep', cc, h_in) * jnp.exp(acum)[..., None]
    return (y_diag + y_off).reshape(bsz, seq, nh, hp)


def ssd_mixer(z, xbc, dt_raw, conv_w, conv_b, dt_bias, a_log, d_skip, norm_g):
    bsz, seq, _ = z.shape
    xbc = jax.nn.silu(causal_depthwise_conv(xbc, conv_w) + conv_b)
    xs, bm, cm = jnp.split(xbc, [SSM_INNER, SSM_INNER + BC_DIM], axis=-1)
    xs = xs.reshape(bsz, seq, SSM_HEADS, SSM_HEAD_DIM).astype(jnp.float32)
    bm = bm.reshape(bsz, seq, SSM_GROUPS, SSM_STATE).astype(jnp.float32)
    cm = cm.reshape(bsz, seq, SSM_GROUPS, SSM_STATE).astype(jnp.float32)
    dt = jax.nn.softplus(dt_raw.astype(jnp.float32) + dt_bias.astype(jnp.float32))
    a = -jnp.exp(a_log.astype(jnp.float32))
    y = ssd_chunked(xs * dt[..., None], dt * a, bm, cm) + d_skip.astype(jnp.float32)[:, None] * xs
    y = y.reshape(bsz, seq, SSM_INNER) * jax.nn.silu(z.astype(jnp.float32))
    return rms_norm(y, norm_g).astype(z.dtype)


def conv_ffn(x, norm_g, w_up, conv_w, conv_b, w_down):
    u = rms_norm(x, norm_g) @ w_up
    u = causal_depthwise_conv(u, conv_w) + conv_b
    gate, val = jnp.split(u, 2, axis=-1)
    return (jax.nn.silu(gate) * val) @ w_down


def setup_inputs(seed: int = 0) -> dict:
    key = jax.random.key(seed)
    ks = jax.random.split(key, 24)
    f32 = jnp.float32

    def nrm(k, shape, fan_in):
        return jax.random.normal(k, shape, f32) * (fan_in ** -0.5)

    def gain(k, shape):
        return 1.0 + 0.02 * jax.random.normal(k, shape, f32)

    dt0 = jnp.exp(jax.random.uniform(ks[8], (DEPTH, SSM_HEADS), f32) * (np.log(0.1) - np.log(0.001)) + np.log(0.001))
    return {
        'x': jax.random.normal(ks[0], (BATCH, SEQ, D_MODEL), f32),
        'p': jax.random.normal(ks[1], (DEPTH, BATCH, SEQ, PLE_DIM), f32),
        'attn_norm_g': gain(ks[2], (DEPTH, D_MODEL)),
        'w_in': nrm(ks[3], (DEPTH, D_MODEL, IN_PROJ_DIM), D_MODEL),
        'q_norm_g': gain(ks[4], (DEPTH, HEAD_DIM)),
        'k_norm_g': gain(ks[5], (DEPTH, HEAD_DIM)),
        'ssm_conv_w': nrm(ks[6], (DEPTH, SSM_CONV, SSM_INNER + 2 * BC_DIM), SSM_CONV),
        'ssm_conv_b': 0.02 * jax.random.normal(ks[7], (DEPTH, SSM_INNER + 2 * BC_DIM), f32),
        'dt_bias': dt0 + jnp.log(-jnp.expm1(-dt0)),
        'a_log': jnp.log(jax.random.uniform(ks[9], (DEPTH, SSM_HEADS), f32, 1.0, 16.0)),
        'd_skip': 1.0 + 0.1 * jax.random.normal(ks[10], (DEPTH, SSM_HEADS), f32),
        'ssm_norm_g': gain(ks[11], (DEPTH, SSM_INNER)),
        'w_out': nrm(ks[12], (DEPTH, MIX_DIM, D_MODEL), MIX_DIM),
        'ffn_norm_g': gain(ks[13], (DEPTH, D_MODEL)),
        'w_up': nrm(ks[14], (DEPTH, D_MODEL, 2 * D_FF), D_MODEL),
        'ffn_conv_w': nrm(ks[15], (DEPTH, FFN_CONV, 2 * D_FF), FFN_CONV),
        'ffn_conv_b': 0.02 * jax.random.normal(ks[16], (DEPTH, 2 * D_FF), f32),
        'w_down': nrm(ks[17], (DEPTH, D_FF, D_MODEL), D_FF),
        'ple_norm_g': gain(ks[18], (DEPTH, D_MODEL)),
        'w_ple_gate': nrm(ks[19], (DEPTH, D_MODEL, D_MODEL), D_MODEL),
        'w_ple_proj': nrm(ks[20], (DEPTH, PLE_DIM, D_MODEL), PLE_DIM),
    }


def reference(x, p, attn_norm_g, w_in, q_norm_g, k_norm_g, ssm_conv_w, ssm_conv_b, dt_bias, a_log,
              d_skip, ssm_norm_g, w_out, ffn_norm_g, w_up, ffn_conv_w, ffn_conv_b, w_down,
              ple_norm_g, w_ple_gate, w_ple_proj):
    splits = [ATTN_DIM, ATTN_DIM + KV_DIM, ATTN_DIM + 2 * KV_DIM, ATTN_DIM + 2 * KV_DIM + SSM_INNER,
              ATTN_DIM + 2 * KV_DIM + 2 * SSM_INNER + 2 * BC_DIM]
    for i in range(DEPTH):
        h = rms_norm(x, attn_norm_g[i])
        proj = h @ w_in[i]
        q, k, v, z, xbc, dt_raw = jnp.split(proj, splits, axis=-1)
        attn_out = dilated_attention(q, k, v, q_norm_g[i], k_norm_g[i])
        ssm_out = ssd_mixer(z, xbc, dt_raw, ssm_conv_w[i], ssm_conv_b[i], dt_bias[i],
                            a_log[i], d_skip[i], ssm_norm_g[i])
        x = x + jnp.concatenate([attn_out, ssm_out], axis=-1) @ w_out[i]
        x = x + conv_ffn(x, ffn_norm_g[i], w_up[i], ffn_conv_w[i], ffn_conv_b[i], w_down[i])
        gate = jax.nn.sigmoid(rms_norm(x, ple_norm_g[i]) @ w_ple_gate[i])
        x = x + gate * (p[i] @ w_ple_proj[i])
    return x
```

```python
import functools

import jax
import jax.numpy as jnp
import numpy as np
from jax import lax
from jax.experimental import pallas as pl
from jax.experimental.pallas import tpu as pltpu

F32 = jnp.float32
BF16 = jnp.bfloat16

D_MODEL = 1024
HEAD_DIM = 64
ATTN_Q_HEADS = 8
ATTN_KV_HEADS = 4
ATTN_GROUP = ATTN_Q_HEADS // ATTN_KV_HEADS
DILATED_PATTERNS = ((128, 1), (512, 4), (2048, 16))
ATTN_BLOCK = 128
ATTN_DIM = ATTN_Q_HEADS * HEAD_DIM
KV_DIM = ATTN_KV_HEADS * HEAD_DIM
SSM_HEADS = 16
SSM_HEAD_DIM = 64
SSM_INNER = SSM_HEADS * SSM_HEAD_DIM
SSM_GROUPS = 2
SSM_STATE = 128
SSM_CONV = 4
SSM_CHUNK = 128
BC_DIM = SSM_GROUPS * SSM_STATE
XBC_DIM = SSM_INNER + 2 * BC_DIM
D_FF = 2816
FFN_CONV = 3
PLE_DIM = 256
EPS = 1e-6

LANES = 128
SUBLANES = 8
QK_DIM = ATTN_DIM + KV_DIM
QKV_DIM = QK_DIM + KV_DIM
VMEM_LIMIT = 56 * 1024 * 1024

IN_TM = 512
TAIL_TM = 256
FF_CHUNK = 256


def _dot(a, b):
    return jnp.dot(a, b, preferred_element_type=F32)


def _dot_nt(a, b):
    return lax.dot_general(a, b, (((1,), (1,)), ((), ())), preferred_element_type=F32)


def _dot_tn(a, b):
    return lax.dot_general(a, b, (((0,), (0,)), ((), ())), preferred_element_type=F32)


def _dot_exact(a, b):
    return jnp.dot(a, b, preferred_element_type=F32, precision=lax.Precision.HIGHEST)


def _rms(x):
    return x * lax.rsqrt(jnp.mean(x * x, axis=-1, keepdims=True) + EPS)


def _silu(x):
    return x * jax.nn.sigmoid(x)


def _const_spec(shape):
    nd = len(shape)
    return pl.BlockSpec(shape, lambda *_: (0,) * nd, pipeline_mode=pl.Buffered(1))


def _in_proj_kernel(tiles_per_batch,
                    x_ref, g_ref, wqk_ref, wv_ref, wz_ref, wxbc_ref, wdt_ref, qkg_ref, bd_ref,
                    cw_ref, cb_ref, dtb_ref,
                    qkv_ref, z_ref, xbc_ref, dt_ref, cbuf):
    i = pl.program_id(0)
    tm = x_ref.shape[0]
    hb = (_rms(x_ref[...]) * g_ref[...]).astype(BF16)

    qk = _dot(hb, wqk_ref[...])
    sq = qk * qk
    sq_hi = sq.astype(BF16)
    sq_lo = (sq - sq_hi.astype(F32)).astype(BF16)
    bd = bd_ref[...]
    for c in range(QK_DIM // 256):
        sl = slice(c * 256, (c + 1) * 256)
        ss = _dot(sq_hi[:, sl], bd) + _dot(sq_lo[:, sl], bd)
        qkn = qk[:, sl] * lax.rsqrt(ss * (1.0 / HEAD_DIM) + EPS) * qkg_ref[:, sl]
        qkv_ref[:, sl] = qkn.astype(BF16)
    qkv_ref[:, QK_DIM:] = _dot(hb, wv_ref[...]).astype(BF16)

    z_ref[...] = _dot(hb, wz_ref[...]).astype(BF16)

    @pl.when(i % tiles_per_batch == 0)
    def _():
        cbuf[0:SUBLANES, :] = jnp.zeros((SUBLANES, XBC_DIM), F32)

    cbuf[SUBLANES:SUBLANES + tm, :] = _dot(hb, wxbc_ref[...])
    acc = cb_ref[...] + cw_ref[SSM_CONV - 1:SSM_CONV, :] * cbuf[SUBLANES:SUBLANES + tm, :]
    for k in range(SSM_CONV - 1):
        off = SUBLANES - (SSM_CONV - 1) + k
        acc = acc + cw_ref[k:k + 1, :] * cbuf[off:off + tm, :]
    xbc_ref[...] = _silu(acc)
    cbuf[0:SUBLANES, :] = cbuf[tm:tm + SUBLANES, :]

    dtr = _dot(hb, wdt_ref[...]) + dtb_ref[...]
    dt_ref[...] = jnp.maximum(dtr, 0.0) + jnp.log1p(jnp.exp(-jnp.abs(dtr)))


def _in_proj(x2d, g, wqk, wv, wz, wxbc, wdt, qkg, bd, cw, cb, dtb, seq):
    t = x2d.shape[0]
    tm = IN_TM
    grid = (t // tm,)
    row = lambda w: pl.BlockSpec((tm, w), lambda i: (i, 0))
    return pl.pallas_call(
        functools.partial(_in_proj_kernel, seq // tm),
        grid=grid,
        in_specs=[row(D_MODEL), _const_spec(g.shape), _const_spec(wqk.shape), _const_spec(wv.shape),
                  _const_spec(wz.shape), _const_spec(wxbc.shape), _const_spec(wdt.shape),
                  _const_spec(qkg.shape), _const_spec(bd.shape), _const_spec(cw.shape),
                  _const_spec(cb.shape), _const_spec(dtb.shape)],
        out_specs=[row(QKV_DIM), row(SSM_INNER), row(XBC_DIM), row(LANES)],
        out_shape=[jax.ShapeDtypeStruct((t, QKV_DIM), BF16), jax.ShapeDtypeStruct((t, SSM_INNER), BF16),
                   jax.ShapeDtypeStruct((t, XBC_DIM), F32), jax.ShapeDtypeStruct((t, LANES), F32)],
        scratch_shapes=[pltpu.VMEM((tm + SUBLANES, XBC_DIM), F32)],
        compiler_params=pltpu.CompilerParams(dimension_semantics=("arbitrary",),
                                             vmem_limit_bytes=VMEM_LIMIT),
        name="in_proj",
    )(x2d, g, wqk, wv, wz, wxbc, wdt, qkg, bd, cw, cb, dtb)


def _attn_kernel(cur_ref, prev_ref, o_ref, m_ref, l_ref):
    n = pl.program_id(1)
    blk = ATTN_BLOCK
    lane = lax.broadcasted_iota(jnp.int32, (blk, LANES), 1)
    lo_half = lane < HEAD_DIM
    qi = lax.broadcasted_iota(jnp.int32, (blk, blk), 0)
    kj = lax.broadcasted_iota(jnp.int32, (blk, blk), 1)
    mask_prev = jnp.logical_and(kj >= qi, n > 0)
    mask_cur = kj <= qi
    neg = -jnp.inf
    for p in range(ATTN_KV_HEADS // 2):
        ls = slice(p * LANES, (p + 1) * LANES)
        kc = cur_ref[0, :, 2 * KV_DIM + p * LANES:2 * KV_DIM + (p + 1) * LANES]
        vc = cur_ref[0, :, 3 * KV_DIM + p * LANES:3 * KV_DIM + (p + 1) * LANES]
        kp = prev_ref[0, :, ls]
        vp = prev_ref[0, :, KV_DIM + p * LANES:KV_DIM + (p + 1) * LANES]
        for g in range(ATTN_GROUP):
            q = cur_ref[0, :, g * KV_DIM + p * LANES:g * KV_DIM + (p + 1) * LANES]
            res = []
            for hh in range(2):
                sel = lo_half if hh == 0 else jnp.logical_not(lo_half)
                qm = jnp.where(sel, q, jnp.zeros_like(q))
                s_prev = jnp.where(mask_prev, _dot_nt(qm, kp), neg)
                s_cur = jnp.where(mask_cur, _dot_nt(qm, kc), neg)
                m = jnp.maximum(jnp.max(s_prev, axis=-1, keepdims=True),
                                jnp.max(s_cur, axis=-1, keepdims=True))
                p_prev = jnp.exp(s_prev - m)
                p_cur = jnp.exp(s_cur - m)
                l = jnp.sum(p_prev, axis=-1, keepdims=True) + jnp.sum(p_cur, axis=-1, keepdims=True)
                o = _dot(p_prev.astype(BF16), vp) + _dot(p_cur.astype(BF16), vc)
                res.append((o, m, l))
            osl = slice(g * KV_DIM + p * LANES, g * KV_DIM + (p + 1) * LANES)
            o_ref[0, :, osl] = jnp.where(lo_half, res[0][0], res[1][0])
            m_ref[0, :, osl] = jnp.where(lo_half, res[0][1], res[1][1])
            l_ref[0, :, osl] = jnp.where(lo_half, res[0][2], res[1][2])


def _attention_pattern(qkv, bsz, seq, dil):
    rows = seq // dil
    nblk = rows // ATTN_BLOCK
    view = qkv.reshape(bsz, rows, dil * QKV_DIM)
    out_sds = jax.ShapeDtypeStruct((bsz, rows, dil * ATTN_DIM), F32)
    out_spec = pl.BlockSpec((1, ATTN_BLOCK, ATTN_DIM), lambda b, n, r: (b, n, r))
    o, m, l = pl.pallas_call(
        _attn_kernel,
        grid=(bsz, nblk, dil),
        in_specs=[pl.BlockSpec((1, ATTN_BLOCK, QKV_DIM), lambda b, n, r: (b, n, r)),
                  pl.BlockSpec((1, ATTN_BLOCK, 2 * KV_DIM),
                               lambda b, n, r: (b, jnp.maximum(n - 1, 0), 2 * r + 1))],
        out_specs=[out_spec, out_spec, out_spec],
        out_shape=[out_sds, out_sds, out_sds],
        compiler_params=pltpu.CompilerParams(dimension_semantics=("arbitrary", "arbitrary", "arbitrary"),
                                             vmem_limit_bytes=VMEM_LIMIT),
        name=f"attn_d{dil}",
    )(view, view)
    t = bsz * seq
    return o.reshape(t, ATTN_DIM), m.reshape(t, ATTN_DIM), l.reshape(t, ATTN_DIM)


def _ssd_kernel(xbc_ref, dt_ref, z_ref, alog_ref, expand_ref, dskip_ref, ng_ref, out_ref, state_ref, y_ref):
    c = pl.program_id(1)
    ln = SSM_CHUNK

    @pl.when(c == 0)
    def _():
        state_ref[...] = jnp.zeros_like(state_ref)

    lane = lax.broadcasted_iota(jnp.int32, (ln, LANES), 1)
    lo_half = lane < SSM_HEAD_DIM
    head_lane = lane < SSM_HEADS
    row = lax.broadcasted_iota(jnp.int32, (ln, ln), 0)
    col = lax.broadcasted_iota(jnp.int32, (ln, ln), 1)
    causal = row >= col
    tril = causal.astype(F32)

    xs = xbc_ref[:, 0:SSM_INNER]
    dt = jnp.where(head_lane, dt_ref[...], 0.0)
    a = dt * (-jnp.exp(alog_ref[...]))
    acum = _dot_exact(tril, a)
    acum_t = lax.dot_general(a, tril, (((0,), (1,)), ((), ())), preferred_element_type=F32,
                             precision=lax.Precision.HIGHEST)
    total = acum[ln - 1:ln, :]
    expand = expand_ref[...]
    dt_e = _dot_exact(dt, expand)
    ea_e = _dot_exact(jnp.exp(acum), expand)
    de_e = _dot_exact(jnp.exp(total - acum), expand)
    xdt = xs * dt_e
    xde = (xdt * de_e).astype(BF16)

    gw = SSM_INNER // SSM_GROUPS
    for g in range(SSM_GROUPS):
        bg = xbc_ref[:, SSM_INNER + g * SSM_STATE:SSM_INNER + (g + 1) * SSM_STATE].astype(BF16)
        cg = xbc_ref[:, SSM_INNER + BC_DIM + g * SSM_STATE:SSM_INNER + BC_DIM + (g + 1) * SSM_STATE].astype(BF16)
        cb = _dot_nt(cg, bg)
        for j in range(SSM_HEADS // SSM_GROUPS // 2):
            h0 = g * (SSM_HEADS // SSM_GROUPS) + 2 * j
            ms = []
            for h in (h0, h0 + 1):
                seg = acum[:, h:h + 1] - acum_t[h:h + 1, :]
                ms.append(cb * jnp.exp(jnp.where(causal, seg, -jnp.inf)))
            mm = jnp.concatenate(ms, axis=1).astype(BF16)
            xp = xdt[:, h0 * SSM_HEAD_DIM:(h0 + 2) * SSM_HEAD_DIM]
            rhs = jnp.concatenate([jnp.where(lo_half, xp, 0.0), jnp.where(lo_half, 0.0, xp)],
                                  axis=0).astype(BF16)
            y_ref[:, h0 * SSM_HEAD_DIM:(h0 + 2) * SSM_HEAD_DIM] = _dot(mm, rhs)
        gs = slice(g * gw, (g + 1) * gw)
        st = state_ref[g]
        y_ref[:, gs] = y_ref[:, gs] + _dot(cg, st.astype(BF16)) * ea_e[:, gs]
        state_ref[g] = st * ea_e[ln - 1:ln, gs] + _dot_tn(bg, xde[:, gs])

    y = y_ref[...] + dskip_ref[...] * xs
    y = y * _silu(z_ref[...].astype(F32))
    out_ref[...] = (_rms(y) * ng_ref[...]).astype(BF16)


def _ssd(xbc, dt, z, alog, expand, dskip, ng, bsz, seq):
    t = bsz * seq
    nc = seq // SSM_CHUNK
    row = lambda w: pl.BlockSpec((SSM_CHUNK, w), lambda b, c: (b * nc + c, 0))
    return pl.pallas_call(
        _ssd_kernel,
        grid=(bsz, nc),
        in_specs=[row(XBC_DIM), row(LANES), row(SSM_INNER), _const_spec(alog.shape),
                  _const_spec(expand.shape), _const_spec(dskip.shape), _const_spec(ng.shape)],
        out_specs=row(SSM_INNER),
        out_shape=jax.ShapeDtypeStruct((t, SSM_INNER), BF16),
        scratch_shapes=[pltpu.VMEM((SSM_GROUPS, SSM_STATE, SSM_INNER // SSM_GROUPS), F32),
                        pltpu.VMEM((SSM_CHUNK, SSM_INNER), F32)],
        compiler_params=pltpu.CompilerParams(dimension_semantics=("arbitrary", "arbitrary"),
                                             vmem_limit_bytes=VMEM_LIMIT),
        name="ssd",
    )(xbc, dt, z, alog, expand, dskip, ng)


def _tail_kernel(tiles_per_batch,
                 x_ref, o1_ref, m1_ref, l1_ref, o4_ref, m4_ref, l4_ref, o16_ref, m16_ref, l16_ref,
                 ssm_ref, p_ref, woa_ref, wos_ref, fg_ref, wup_ref, fcw_ref, fcb_ref, wdn_ref,
                 pg_ref, wpg_ref, wpp_ref, out_ref, ubuf):
    i = pl.program_id(0)
    tm = x_ref.shape[0]

    m1, m4, m16 = m1_ref[...], m4_ref[...], m16_ref[...]
    mx = jnp.maximum(jnp.maximum(m1, m4), m16)
    w1, w4, w16 = jnp.exp(m1 - mx), jnp.exp(m4 - mx), jnp.exp(m16 - mx)
    num = w1 * o1_ref[...] + w4 * o4_ref[...] + w16 * o16_ref[...]
    den = w1 * l1_ref[...] + w4 * l4_ref[...] + w16 * l16_ref[...]
    attn = (num / den).astype(BF16)

    x1 = x_ref[...] + _dot(attn, woa_ref[...]) + _dot(ssm_ref[...], wos_ref[...])

    h2 = (_rms(x1) * fg_ref[...]).astype(BF16)

    @pl.when(i % tiles_per_batch == 0)
    def _():
        ubuf[0:SUBLANES, :] = jnp.zeros((SUBLANES, 2 * D_FF), F32)

    def conv(cs):
        ubuf[SUBLANES:SUBLANES + tm, cs] = _dot(h2, wup_ref[:, cs])
        acc = fcb_ref[:, cs] + fcw_ref[FFN_CONV - 1:FFN_CONV, cs] * ubuf[SUBLANES:SUBLANES + tm, cs]
        for k in range(FFN_CONV - 1):
            off = SUBLANES - (FFN_CONV - 1) + k
            acc = acc + fcw_ref[k:k + 1, cs] * ubuf[off:off + tm, cs]
        return acc

    ffn = jnp.zeros((tm, D_MODEL), F32)
    for c in range(D_FF // FF_CHUNK):
        gate = conv(slice(c * FF_CHUNK, (c + 1) * FF_CHUNK))
        val = conv(slice(D_FF + c * FF_CHUNK, D_FF + (c + 1) * FF_CHUNK))
        act = (_silu(gate) * val).astype(BF16)
        ffn = ffn + _dot(act, wdn_ref[c * FF_CHUNK:(c + 1) * FF_CHUNK, :])
    ubuf[0:SUBLANES, :] = ubuf[tm:tm + SUBLANES, :]
    x2 = x1 + ffn

    h3 = (_rms(x2) * pg_ref[...]).astype(BF16)
    gate = jax.nn.sigmoid(_dot(h3, wpg_ref[...]))
    out_ref[...] = x2 + gate * _dot(p_ref[...].astype(BF16), wpp_ref[...])


def _tail(x2d, attn_parts, ssm, p2d, woa, wos, fg, wup, fcw, fcb, wdn, pg, wpg, wpp, seq):
    t = x2d.shape[0]
    tm = TAIL_TM
    row = lambda w: pl.BlockSpec((tm, w), lambda i: (i, 0))
    consts = [woa, wos, fg, wup, fcw, fcb, wdn, pg, wpg, wpp]
    return pl.pallas_call(
        functools.partial(_tail_kernel, seq // tm),
        grid=(t // tm,),
        in_specs=[row(D_MODEL)] + [row(ATTN_DIM)] * 9 + [row(SSM_INNER), row(PLE_DIM)]
                 + [_const_spec(w.shape) for w in consts],
        out_specs=row(D_MODEL),
        out_shape=jax.ShapeDtypeStruct((t, D_MODEL), F32),
        scratch_shapes=[pltpu.VMEM((tm + SUBLANES, 2 * D_FF), F32)],
        compiler_params=pltpu.CompilerParams(dimension_semantics=("arbitrary",),
                                             vmem_limit_bytes=VMEM_LIMIT),
        name="tail",
    )(x2d, *attn_parts, ssm, p2d, *consts)


def _q_perm():
    idx = np.arange(ATTN_DIM).reshape(ATTN_KV_HEADS, ATTN_GROUP, HEAD_DIM)
    return np.transpose(idx, (1, 0, 2)).reshape(-1)


def _layer(x, p, attn_norm_g, w_in, q_norm_g, k_norm_g, ssm_conv_w, ssm_conv_b, dt_bias, a_log,
           d_skip, ssm_norm_g, w_out, ffn_norm_g, w_up, ffn_conv_w, ffn_conv_b, w_down,
           ple_norm_g, w_ple_gate, w_ple_proj):
    bsz, seq, _ = x.shape
    t = bsz * seq
    x2d = x.reshape(t, D_MODEL)
    p2d = p.reshape(t, PLE_DIM)
    qperm = _q_perm()

    o_k, o_v, o_z = ATTN_DIM, ATTN_DIM + KV_DIM, ATTN_DIM + 2 * KV_DIM
    o_xbc, o_dt = o_z + SSM_INNER, o_z + SSM_INNER + XBC_DIM
    wqk = jnp.concatenate([w_in[:, :ATTN_DIM][:, qperm], w_in[:, o_k:o_v]], axis=1).astype(BF16)
    wv = w_in[:, o_v:o_z].astype(BF16)
    wz = w_in[:, o_z:o_xbc].astype(BF16)
    wxbc = w_in[:, o_xbc:o_dt].astype(BF16)
    wdt = jnp.pad(w_in[:, o_dt:], ((0, 0), (0, LANES - SSM_HEADS))).astype(BF16)
    qkg = jnp.concatenate([jnp.tile(q_norm_g * (HEAD_DIM ** -0.5), ATTN_Q_HEADS),
                           jnp.tile(k_norm_g, ATTN_KV_HEADS)]).reshape(1, QK_DIM)
    hid = np.arange(256) // HEAD_DIM
    bd = jnp.asarray(hid[:, None] == hid[None, :], BF16)
    dtb = jnp.pad(dt_bias, (0, LANES - SSM_HEADS)).reshape(1, LANES)
    alog = jnp.pad(a_log, (0, LANES - SSM_HEADS)).reshape(1, LANES)

    qkv, z, xbc, dt = _in_proj(x2d, attn_norm_g.reshape(1, D_MODEL), wqk, wv, wz, wxbc, wdt, qkg, bd,
                               ssm_conv_w, ssm_conv_b.reshape(1, XBC_DIM), dtb, seq)

    attn_parts = []
    for (_, dil) in DILATED_PATTERNS:
        attn_parts.extend(_attention_pattern(qkv, bsz, seq, dil))

    expand = jnp.asarray(np.arange(LANES)[:, None] == (np.arange(SSM_INNER) // SSM_HEAD_DIM)[None, :], F32)
    dskip = jnp.repeat(d_skip, SSM_HEAD_DIM).reshape(1, SSM_INNER)
    ssm = _ssd(xbc, dt, z, alog, expand, dskip, ssm_norm_g.reshape(1, SSM_INNER), bsz, seq)

    woa = w_out[:ATTN_DIM][qperm].astype(BF16)
    wos = w_out[ATTN_DIM:].astype(BF16)
    out = _tail(x2d, attn_parts, ssm, p2d, woa, wos, ffn_norm_g.reshape(1, D_MODEL), w_up.astype(BF16),
                ffn_conv_w, ffn_conv_b.reshape(1, 2 * D_FF), w_down.astype(BF16),
                ple_norm_g.reshape(1, D_MODEL), w_ple_gate.astype(BF16), w_ple_proj.astype(BF16), seq)
    return out.reshape(bsz, seq, D_MODEL)


def kernel(x, p, attn_norm_g, w_in, q_norm_g, k_norm_g, ssm_conv_w, ssm_conv_b, dt_bias, a_log, d_skip,
           ssm_norm_g, w_out, ffn_norm_g, w_up, ffn_conv_w, ffn_conv_b, w_down, ple_norm_g, w_ple_gate,
           w_ple_proj):
    depth = w_in.shape[0]
    for i in range(depth):
        x = _layer(x, p[i], attn_norm_g[i], w_in[i], q_norm_g[i], k_norm_g[i], ssm_conv_w[i], ssm_conv_b[i],
                   dt_bias[i], a_log[i], d_skip[i], ssm_norm_g[i], w_out[i], ffn_norm_g[i], w_up[i],
                   ffn_conv_w[i], ffn_conv_b[i], w_down[i], ple_norm_g[i], w_ple_gate[i], w_ple_proj[i])
    return x
```

```python
import functools

import jax
import jax.numpy as jnp
import numpy as np
from jax import lax
from jax.experimental import pallas as pl
from jax.experimental.pallas import tpu as pltpu

F32 = jnp.float32
BF16 = jnp.bfloat16

D_MODEL = 1024
HEAD_DIM = 64
ATTN_Q_HEADS = 8
ATTN_KV_HEADS = 4
ATTN_GROUP = ATTN_Q_HEADS // ATTN_KV_HEADS
DILATED_PATTERNS = ((128, 1), (512, 4), (2048, 16))
ATTN_BLOCK = 128
ATTN_DIM = ATTN_Q_HEADS * HEAD_DIM
KV_DIM = ATTN_KV_HEADS * HEAD_DIM
SSM_HEADS = 16
SSM_HEAD_DIM = 64
SSM_INNER = SSM_HEADS * SSM_HEAD_DIM
SSM_GROUPS = 2
SSM_STATE = 128
SSM_CONV = 4
SSM_CHUNK = 128
BC_DIM = SSM_GROUPS * SSM_STATE
XBC_DIM = SSM_INNER + 2 * BC_DIM
D_FF = 2816
FFN_CONV = 3
PLE_DIM = 256
EPS = 1e-6

LANES = 128
SUBLANES = 8
QK_DIM = ATTN_DIM + KV_DIM
QKV_DIM = QK_DIM + KV_DIM
QKV_SLABS = QKV_DIM // LANES
ATTN_SLABS = ATTN_DIM // LANES
KV_PAIRS = ATTN_KV_HEADS // 2
VMEM_LIMIT = 56 * 1024 * 1024

MAX_DIL = 16
SUPER = MAX_DIL * ATTN_BLOCK
IN_TM = 512
TAIL_TM = 256
FF_CHUNK = 256

assert DILATED_PATTERNS == ((128, 1), (512, 4), (2048, 16))
assert all(w // d == ATTN_BLOCK for w, d in DILATED_PATTERNS)


def _dot(a, b):
    return jnp.dot(a, b, preferred_element_type=F32)


def _dot_nt(a, b):
    return lax.dot_general(a, b, (((1,), (1,)), ((), ())), preferred_element_type=F32)


def _dot_tn(a, b):
    return lax.dot_general(a, b, (((0,), (0,)), ((), ())), preferred_element_type=F32)


def _dot_exact(a, b):
    return jnp.dot(a, b, preferred_element_type=F32, precision=lax.Precision.HIGHEST)


def _rms(x):
    return x * lax.rsqrt(jnp.mean(x * x, axis=-1, keepdims=True) + EPS)


def _silu(x):
    return x * jax.nn.sigmoid(x)


def _const_spec(shape):
    nd = len(shape)
    return pl.BlockSpec(shape, lambda *_: (0,) * nd, pipeline_mode=pl.Buffered(1))


def _in_proj_kernel(tiles_per_batch,
                    x_ref, g_ref, wqk_ref, wv_ref, wz_ref, wxbc_ref, wdt_ref, qkg_ref, bd_ref,
                    cw_ref, cb_ref, dtb_ref,
                    qkv_ref, z_ref, xbc_ref, dt_ref, cbuf, nat, mod4):
    i = pl.program_id(0)
    tm = x_ref.shape[0]
    hb = (_rms(x_ref[...]) * g_ref[...]).astype(BF16)

    qk = _dot(hb, wqk_ref[...])
    sq = qk * qk
    sq_hi = sq.astype(BF16)
    sq_lo = (sq - sq_hi.astype(F32)).astype(BF16)
    bd = bd_ref[...]
    for c in range(QK_DIM // 256):
        sl = slice(c * 256, (c + 1) * 256)
        ss = _dot(sq_hi[:, sl], bd) + _dot(sq_lo[:, sl], bd)
        qkn = qk[:, sl] * lax.rsqrt(ss * (1.0 / HEAD_DIM) + EPS) * qkg_ref[:, sl]
        nat[2 * c] = qkn[:, :LANES]
        nat[2 * c + 1] = qkn[:, LANES:]
    v = _dot(hb, wv_ref[...])
    nat[QKV_SLABS - 2] = v[:, :LANES]
    nat[QKV_SLABS - 1] = v[:, LANES:]

    quarter = tm // 4
    sixteenth = tm // MAX_DIL
    for s in range(QKV_SLABS):
        for a in range(4):
            mod4[s, a * quarter:(a + 1) * quarter, :] = nat[s, pl.ds(a, quarter, stride=4), :]
    k = i % (SUPER // tm)
    for s in range(QKV_SLABS):
        for a in range(4):
            for b in range(4):
                dst = pl.multiple_of((4 * b + a) * ATTN_BLOCK + sixteenth * k, sixteenth)
                qkv_ref[s, pl.ds(dst, sixteenth), :] = mod4[s, pl.ds(a * quarter + b, sixteenth, stride=4), :]

    z_ref[...] = _dot(hb, wz_ref[...]).astype(BF16)

    @pl.when(i % tiles_per_batch == 0)
    def _():
        cbuf[0:SUBLANES, :] = jnp.zeros((SUBLANES, XBC_DIM), F32)

    cbuf[SUBLANES:SUBLANES + tm, :] = _dot(hb, wxbc_ref[...])
    acc = cb_ref[...] + cw_ref[SSM_CONV - 1:SSM_CONV, :] * cbuf[SUBLANES:SUBLANES + tm, :]
    for kk in range(SSM_CONV - 1):
        off = SUBLANES - (SSM_CONV - 1) + kk
        acc = acc + cw_ref[kk:kk + 1, :] * cbuf[off:off + tm, :]
    xbc_ref[...] = _silu(acc)
    cbuf[0:SUBLANES, :] = cbuf[tm:tm + SUBLANES, :]

    dtr = _dot(hb, wdt_ref[...]) + dtb_ref[...]
    dt_ref[...] = jnp.maximum(dtr, 0.0) + jnp.log1p(jnp.exp(-jnp.abs(dtr)))


def _in_proj(x2d, g, wqk, wv, wz, wxbc, wdt, qkg, bd, cw, cb, dtb, seq):
    t = x2d.shape[0]
    tm = IN_TM
    assert SUPER % tm == 0 and tm % (4 * MAX_DIL) == 0 and seq % SUPER == 0
    grid = (t // tm,)
    row = lambda w: pl.BlockSpec((tm, w), lambda i: (i, 0))
    per_super = SUPER // tm
    return pl.pallas_call(
        functools.partial(_in_proj_kernel, seq // tm),
        grid=grid,
        in_specs=[row(D_MODEL), _const_spec(g.shape), _const_spec(wqk.shape), _const_spec(wv.shape),
                  _const_spec(wz.shape), _const_spec(wxbc.shape), _const_spec(wdt.shape),
                  _const_spec(qkg.shape), _const_spec(bd.shape), _const_spec(cw.shape),
                  _const_spec(cb.shape), _const_spec(dtb.shape)],
        out_specs=[pl.BlockSpec((QKV_SLABS, SUPER, LANES), lambda i: (0, i // per_super, 0)),
                   row(SSM_INNER), row(XBC_DIM), row(LANES)],
        out_shape=[jax.ShapeDtypeStruct((QKV_SLABS, t, LANES), F32), jax.ShapeDtypeStruct((t, SSM_INNER), BF16),
                   jax.ShapeDtypeStruct((t, XBC_DIM), F32), jax.ShapeDtypeStruct((t, LANES), F32)],
        scratch_shapes=[pltpu.VMEM((tm + SUBLANES, XBC_DIM), F32),
                        pltpu.VMEM((QKV_SLABS, tm, LANES), F32),
                        pltpu.VMEM((QKV_SLABS, tm, LANES), F32)],
        compiler_params=pltpu.CompilerParams(dimension_semantics=("arbitrary",),
                                             vmem_limit_bytes=VMEM_LIMIT),
        name="in_proj",
    )(x2d, g, wqk, wv, wz, wxbc, wdt, qkg, bd, cw, cb, dtb)


def _block_perm(dil):
    reps = MAX_DIL // dil
    per = ATTN_BLOCK // reps
    pos = np.arange(ATTN_BLOCK)
    return (pos % per) * reps + pos // per


def _attn_bias():
    out = []
    for _, dil in DILATED_PATTERNS[::-1]:
        perm = _block_perm(dil)
        qi = perm[:, None]
        kj = np.concatenate([perm, ATTN_BLOCK + perm])[None, :]
        delta = ATTN_BLOCK + qi - kj
        band = (delta >= 0) & (delta <= ATTN_BLOCK)
        for first in (False, True):
            ok = band & (kj >= ATTN_BLOCK) if first else band
            out.append(np.tile(np.where(ok, 0.0, -np.inf).astype(np.float32), (4, 1)))
    return np.stack(out)


def _attn_kernel(cur_ref, prev_ref, bias_ref, out_ref, acc_o, acc_m, acc_l):
    first = (pl.program_id(1) == 0).astype(jnp.int32)
    lane = lax.broadcasted_iota(jnp.int32, (ATTN_BLOCK, LANES), 1)
    lo_half = lane < HEAD_DIM
    ones = jnp.ones((2 * ATTN_BLOCK, LANES), BF16)
    k_slab, v_slab = ATTN_SLABS, ATTN_SLABS + KV_PAIRS

    def rows(ref, slab, pieces):
        parts = [ref[slab, pl.ds(st, sz), :] for st, sz in pieces]
        return parts[0] if len(parts) == 1 else jnp.concatenate(parts, axis=0)

    def unit(q_pieces, prev_pieces, prev_in_cur, bias_idx, mode, out_start=None):
        bias = bias_ref[bias_idx]
        for p in range(KV_PAIRS):
            if prev_in_cur:
                kp, vp = rows(cur_ref, k_slab + p, prev_pieces), rows(cur_ref, v_slab + p, prev_pieces)
            else:
                kp, vp = rows(prev_ref, p, prev_pieces), rows(prev_ref, KV_PAIRS + p, prev_pieces)
            k2 = jnp.concatenate([kp, rows(cur_ref, k_slab + p, q_pieces)], axis=0).astype(BF16)
            v2 = jnp.concatenate([vp, rows(cur_ref, v_slab + p, q_pieces)], axis=0).astype(BF16)
            v2 = jnp.concatenate([v2, ones], axis=1)
            qs = []
            for g in range(ATTN_GROUP):
                q = rows(cur_ref, g * KV_PAIRS + p, q_pieces)
                qs += [jnp.where(lo_half, q, 0.0), jnp.where(lo_half, 0.0, q)]
            q4 = jnp.concatenate(qs, axis=0).astype(BF16)
            s = _dot_nt(q4, k2) + bias
            m = jnp.max(s, axis=-1, keepdims=True)
            pr = jnp.exp(s - m).astype(BF16)
            ol = _dot(pr, v2)
            for g in range(ATTN_GROUP):
                r0, r1 = (2 * g) * ATTN_BLOCK, (2 * g + 1) * ATTN_BLOCK
                o = jnp.where(lo_half, ol[r0:r0 + ATTN_BLOCK, :LANES], ol[r1:r1 + ATTN_BLOCK, :LANES])
                l = jnp.where(lo_half, ol[r0:r0 + ATTN_BLOCK, LANES:], ol[r1:r1 + ATTN_BLOCK, LANES:])
                mm = jnp.where(lo_half, m[r0:r0 + ATTN_BLOCK], m[r1:r1 + ATTN_BLOCK])
                slab = g * KV_PAIRS + p
                if mode != "init":
                    m_old = rows(acc_m, slab, q_pieces)
                    m_new = jnp.maximum(m_old, mm)
                    w_old, w_new = jnp.exp(m_old - m_new), jnp.exp(mm - m_new)
                    o = rows(acc_o, slab, q_pieces) * w_old + o * w_new
                    l = rows(acc_l, slab, q_pieces) * w_old + l * w_new
                    mm = m_new
                if mode == "final":
                    res = o / l
                    for r in range(MAX_DIL):
                        out_ref[slab, pl.ds(out_start + r, SUBLANES, stride=MAX_DIL), :] = (
                            res[r * SUBLANES:(r + 1) * SUBLANES])
                else:
                    off = 0
                    for st, sz in q_pieces:
                        acc_o[slab, pl.ds(st, sz), :] = o[off:off + sz]
                        acc_m[slab, pl.ds(st, sz), :] = mm[off:off + sz]
                        acc_l[slab, pl.ds(st, sz), :] = l[off:off + sz]
                        off += sz

    def aligned(x, m):
        return x if isinstance(x, int) else pl.multiple_of(x, m)

    def d16_body(r, carry):
        st = pl.multiple_of(r * ATTN_BLOCK, ATTN_BLOCK)
        unit([(st, ATTN_BLOCK)], [(st, ATTN_BLOCK)], False, first, "init")
        return carry
    lax.fori_loop(0, MAX_DIL, d16_body, 0)

    def d4_pieces(c, row0):
        return [(aligned((4 * a + c) * ATTN_BLOCK + row0, 32), 32) for a in range(4)]

    def d4_body(c, carry):
        unit(d4_pieces(c, 0), d4_pieces(c, ATTN_BLOCK - 32), False, 2 + first, "merge")

        def inner(n, carry2):
            unit(d4_pieces(c, 32 * n), d4_pieces(c, 32 * (n - 1)), True, 2, "merge")
            return carry2
        return lax.fori_loop(1, 4, inner, carry)
    lax.fori_loop(0, 4, d4_body, 0)

    def d1_pieces(row0):
        return [(aligned(r * ATTN_BLOCK + row0, SUBLANES), SUBLANES) for r in range(MAX_DIL)]

    unit(d1_pieces(0), d1_pieces(ATTN_BLOCK - SUBLANES), False, 4 + first, "final", 0)

    def d1_body(n, carry):
        unit(d1_pieces(SUBLANES * n), d1_pieces(SUBLANES * (n - 1)), True, 4, "final", n * ATTN_BLOCK)
        return carry
    lax.fori_loop(1, SUPER // ATTN_BLOCK, d1_body, 0)


def _attention(qkv, bias, bsz, seq):
    t = bsz * seq
    nsb = seq // SUPER
    acc = pltpu.VMEM((ATTN_SLABS, SUPER, LANES), F32)
    return pl.pallas_call(
        _attn_kernel,
        grid=(bsz, nsb),
        in_specs=[pl.BlockSpec((QKV_SLABS, SUPER, LANES), lambda b, s: (0, b * nsb + s, 0)),
                  pl.BlockSpec((QKV_SLABS // 2, SUPER, LANES),
                               lambda b, s: (1, b * nsb + jnp.maximum(s - 1, 0), 0)),
                  _const_spec(bias.shape)],
        out_specs=pl.BlockSpec((ATTN_SLABS, SUPER, LANES), lambda b, s: (0, b * nsb + s, 0)),
        out_shape=jax.ShapeDtypeStruct((ATTN_SLABS, t, LANES), F32),
        scratch_shapes=[acc, acc, acc],
        compiler_params=pltpu.CompilerParams(dimension_semantics=("arbitrary", "arbitrary"),
                                             vmem_limit_bytes=VMEM_LIMIT),
        name="attention",
    )(qkv, qkv, bias)


def _ssd_kernel(xbc_ref, dt_ref, z_ref, alog_ref, expand_ref, dskip_ref, ng_ref, out_ref, state_ref, y_ref):
    c = pl.program_id(1)
    ln = SSM_CHUNK

    @pl.when(c == 0)
    def _():
        state_ref[...] = jnp.zeros_like(state_ref)

    lane = lax.broadcasted_iota(jnp.int32, (ln, LANES), 1)
    lo_half = lane < SSM_HEAD_DIM
    head_lane = lane < SSM_HEADS
    row = lax.broadcasted_iota(jnp.int32, (ln, ln), 0)
    col = lax.broadcasted_iota(jnp.int32, (ln, ln), 1)
    causal = row >= col
    tril = causal.astype(F32)

    xs = xbc_ref[:, 0:SSM_INNER]
    dt = jnp.where(head_lane, dt_ref[...], 0.0)
    a = dt * (-jnp.exp(alog_ref[...]))
    acum = _dot_exact(tril, a)
    acum_t = lax.dot_general(a, tril, (((0,), (1,)), ((), ())), preferred_element_type=F32,
                             precision=lax.Precision.HIGHEST)
    total = acum[ln - 1:ln, :]
    expand = expand_ref[...]
    dt_e = _dot_exact(dt, expand)
    ea_e = _dot_exact(jnp.exp(acum), expand)
    de_e = _dot_exact(jnp.exp(total - acum), expand)
    xdt = xs * dt_e
    xde = (xdt * de_e).astype(BF16)

    gw = SSM_INNER // SSM_GROUPS
    for g in range(SSM_GROUPS):
        bg = xbc_ref[:, SSM_INNER + g * SSM_STATE:SSM_INNER + (g + 1) * SSM_STATE].astype(BF16)
        cg = xbc_ref[:, SSM_INNER + BC_DIM + g * SSM_STATE:SSM_INNER + BC_DIM + (g + 1) * SSM_STATE].astype(BF16)
        cb = _dot_nt(cg, bg)
        for j in range(SSM_HEADS // SSM_GROUPS // 2):
            h0 = g * (SSM_HEADS // SSM_GROUPS) + 2 * j
            ms = []
            for h in (h0, h0 + 1):
                seg = acum[:, h:h + 1] - acum_t[h:h + 1, :]
                ms.append(cb * jnp.exp(jnp.where(causal, seg, -jnp.inf)))
            mm = jnp.concatenate(ms, axis=1).astype(BF16)
            xp = xdt[:, h0 * SSM_HEAD_DIM:(h0 + 2) * SSM_HEAD_DIM]
            rhs = jnp.concatenate([jnp.where(lo_half, xp, 0.0), jnp.where(lo_half, 0.0, xp)],
                                  axis=0).astype(BF16)
            y_ref[:, h0 * SSM_HEAD_DIM:(h0 + 2) * SSM_HEAD_DIM] = _dot(mm, rhs)
        gs = slice(g * gw, (g + 1) * gw)
        st = state_ref[g]
        y_ref[:, gs] = y_ref[:, gs] + _dot(cg, st.astype(BF16)) * ea_e[:, gs]
        state_ref[g] = st * ea_e[ln - 1:ln, gs] + _dot_tn(bg, xde[:, gs])

    y = y_ref[...] + dskip_ref[...] * xs
    y = y * _silu(z_ref[...].astype(F32))
    out_ref[...] = (_rms(y) * ng_ref[...]).astype(BF16)


def _ssd(xbc, dt, z, alog, expand, dskip, ng, bsz, seq):
    t = bsz * seq
    nc = seq // SSM_CHUNK
    row = lambda w: pl.BlockSpec((SSM_CHUNK, w), lambda b, c: (b * nc + c, 0))
    return pl.pallas_call(
        _ssd_kernel,
        grid=(bsz, nc),
        in_specs=[row(XBC_DIM), row(LANES), row(SSM_INNER), _const_spec(alog.shape),
                  _const_spec(expand.shape), _const_spec(dskip.shape), _const_spec(ng.shape)],
        out_specs=row(SSM_INNER),
        out_shape=jax.ShapeDtypeStruct((t, SSM_INNER), BF16),
        scratch_shapes=[pltpu.VMEM((SSM_GROUPS, SSM_STATE, SSM_INNER // SSM_GROUPS), F32),
                        pltpu.VMEM((SSM_CHUNK, SSM_INNER), F32)],
        compiler_params=pltpu.CompilerParams(dimension_semantics=("arbitrary", "arbitrary"),
                                             vmem_limit_bytes=VMEM_LIMIT),
        name="ssd",
    )(xbc, dt, z, alog, expand, dskip, ng)


def _tail_kernel(tiles_per_batch,
                 x_ref, attn_ref, ssm_ref, p_ref, woa_ref, wos_ref, fg_ref, wup_ref, fcw_ref, fcb_ref, wdn_ref,
                 pg_ref, wpg_ref, wpp_ref, out_ref, ubuf):
    i = pl.program_id(0)
    tm = x_ref.shape[0]

    x1 = x_ref[...] + _dot(ssm_ref[...], wos_ref[...])
    for s in range(ATTN_SLABS):
        x1 = x1 + _dot(attn_ref[s].astype(BF16), woa_ref[s * LANES:(s + 1) * LANES, :])

    h2 = (_rms(x1) * fg_ref[...]).astype(BF16)

    @pl.when(i % tiles_per_batch == 0)
    def _():
        ubuf[0:SUBLANES, :] = jnp.zeros((SUBLANES, 2 * D_FF), F32)

    def conv(cs):
        ubuf[SUBLANES:SUBLANES + tm, cs] = _dot(h2, wup_ref[:, cs])
        acc = fcb_ref[:, cs] + fcw_ref[FFN_CONV - 1:FFN_CONV, cs] * ubuf[SUBLANES:SUBLANES + tm, cs]
        for k in range(FFN_CONV - 1):
            off = SUBLANES - (FFN_CONV - 1) + k
            acc = acc + fcw_ref[k:k + 1, cs] * ubuf[off:off + tm, cs]
        return acc

    ffn = jnp.zeros((tm, D_MODEL), F32)
    for c in range(D_FF // FF_CHUNK):
        gate = conv(slice(c * FF_CHUNK, (c + 1) * FF_CHUNK))
        val = conv(slice(D_FF + c * FF_CHUNK, D_FF + (c + 1) * FF_CHUNK))
        act = (_silu(gate) * val).astype(BF16)
        ffn = ffn + _dot(act, wdn_ref[c * FF_CHUNK:(c + 1) * FF_CHUNK, :])
    ubuf[0:SUBLANES, :] = ubuf[tm:tm + SUBLANES, :]
    x2 = x1 + ffn

    h3 = (_rms(x2) * pg_ref[...]).astype(BF16)
    gate = jax.nn.sigmoid(_dot(h3, wpg_ref[...]))
    out_ref[...] = x2 + gate * _dot(p_ref[...].astype(BF16), wpp_ref[...])


def _tail(x2d, attn, ssm, p2d, woa, wos, fg, wup, fcw, fcb, wdn, pg, wpg, wpp, seq):
    t = x2d.shape[0]
    tm = TAIL_TM
    row = lambda w: pl.BlockSpec((tm, w), lambda i: (i, 0))
    consts = [woa, wos, fg, wup, fcw, fcb, wdn, pg, wpg, wpp]
    return pl.pallas_call(
        functools.partial(_tail_kernel, seq // tm),
        grid=(t // tm,),
        in_specs=[row(D_MODEL), pl.BlockSpec((ATTN_SLABS, tm, LANES), lambda i: (0, i, 0)),
                  row(SSM_INNER), row(PLE_DIM)] + [_const_spec(w.shape) for w in consts],
        out_specs=row(D_MODEL),
        out_shape=jax.ShapeDtypeStruct((t, D_MODEL), F32),
        scratch_shapes=[pltpu.VMEM((tm + SUBLANES, 2 * D_FF), F32)],
        compiler_params=pltpu.CompilerParams(dimension_semantics=("arbitrary",),
                                             vmem_limit_bytes=VMEM_LIMIT),
        name="tail",
    )(x2d, attn, ssm, p2d, *consts)


def _q_perm():
    idx = np.arange(ATTN_DIM).reshape(ATTN_KV_HEADS, ATTN_GROUP, HEAD_DIM)
    return np.transpose(idx, (1, 0, 2)).reshape(-1)


def _layer(x, p, attn_norm_g, w_in, q_norm_g, k_norm_g, ssm_conv_w, ssm_conv_b, dt_bias, a_log,
           d_skip, ssm_norm_g, w_out, ffn_norm_g, w_up, ffn_conv_w, ffn_conv_b, w_down,
           ple_norm_g, w_ple_gate, w_ple_proj):
    bsz, seq, _ = x.shape
    t = bsz * seq
    x2d = x.reshape(t, D_MODEL)
    p2d = p.reshape(t, PLE_DIM)
    qperm = _q_perm()

    o_k, o_v, o_z = ATTN_DIM, ATTN_DIM + KV_DIM, ATTN_DIM + 2 * KV_DIM
    o_xbc, o_dt = o_z + SSM_INNER, o_z + SSM_INNER + XBC_DIM
    wqk = jnp.concatenate([w_in[:, :ATTN_DIM][:, qperm], w_in[:, o_k:o_v]], axis=1).astype(BF16)
    wv = w_in[:, o_v:o_z].astype(BF16)
    wz = w_in[:, o_z:o_xbc].astype(BF16)
    wxbc = w_in[:, o_xbc:o_dt].astype(BF16)
    wdt = jnp.pad(w_in[:, o_dt:], ((0, 0), (0, LANES - SSM_HEADS))).astype(BF16)
    qkg = jnp.concatenate([jnp.tile(q_norm_g * (HEAD_DIM ** -0.5), ATTN_Q_HEADS),
                           jnp.tile(k_norm_g, ATTN_KV_HEADS)]).reshape(1, QK_DIM)
    hid = np.arange(256) // HEAD_DIM
    bd = jnp.asarray(hid[:, None] == hid[None, :], BF16)
    dtb = jnp.pad(dt_bias, (0, LANES - SSM_HEADS)).reshape(1, LANES)
    alog = jnp.pad(a_log, (0, LANES - SSM_HEADS)).reshape(1, LANES)

    qkv, z, xbc, dt = _in_proj(x2d, attn_norm_g.reshape(1, D_MODEL), wqk, wv, wz, wxbc, wdt, qkg, bd,
                               ssm_conv_w, ssm_conv_b.reshape(1, XBC_DIM), dtb, seq)

    attn = _attention(qkv, jnp.asarray(_attn_bias()), bsz, seq)

    expand = jnp.asarray(np.arange(LANES)[:, None] == (np.arange(SSM_INNER) // SSM_HEAD_DIM)[None, :], F32)
    dskip = jnp.repeat(d_skip, SSM_HEAD_DIM).reshape(1, SSM_INNER)
    ssm = _ssd(xbc, dt, z, alog, expand, dskip, ssm_norm_g.reshape(1, SSM_INNER), bsz, seq)

    woa = w_out[:ATTN_DIM][qperm].astype(BF16)
    wos = w_out[ATTN_DIM:].astype(BF16)
    out = _tail(x2d, attn, ssm, p2d, woa, wos, ffn_norm_g.reshape(1, D_MODEL), w_up.astype(BF16),
                ffn_conv_w, ffn_conv_b.reshape(1, 2 * D_FF), w_down.astype(BF16),
                ple_norm_g.reshape(1, D_MODEL), w_ple_gate.astype(BF16), w_ple_proj.astype(BF16), seq)
    return out.reshape(bsz, seq, D_MODEL)


def kernel(x, p, attn_norm_g, w_in, q_norm_g, k_norm_g, ssm_conv_w, ssm_conv_b, dt_bias, a_log, d_skip,
           ssm_norm_g, w_out, ffn_norm_g, w_up, ffn_conv_w, ffn_conv_b, w_down, ple_norm_g, w_ple_gate,
           w_ple_proj):
    depth = w_in.shape[0]
    for i in range(depth):
        x = _layer(x, p[i], attn_norm_g[i], w_in[i], q_norm_g[i], k_norm_g[i], ssm_conv_w[i], ssm_conv_b[i],
                   dt_bias[i], a_log[i], d_skip[i], ssm_norm_g[i], w_out[i], ffn_norm_g[i], w_up[i],
                   ffn_conv_w[i], ffn_conv_b[i], w_down[i], ple_norm_g[i], w_ple_gate[i], w_ple_proj[i])
    return x
```

```python
import functools

import jax
import jax.numpy as jnp
import numpy as np
from jax import lax
from jax.experimental import pallas as pl
from jax.experimental.pallas import tpu as pltpu

F32 = jnp.float32
BF16 = jnp.bfloat16

D_MODEL = 1024
HEAD_DIM = 64
ATTN_Q_HEADS = 8
ATTN_KV_HEADS = 4
ATTN_GROUP = ATTN_Q_HEADS // ATTN_KV_HEADS
DILATED_PATTERNS = ((128, 1), (512, 4), (2048, 16))
ATTN_BLOCK = 128
ATTN_DIM = ATTN_Q_HEADS * HEAD_DIM
KV_DIM = ATTN_KV_HEADS * HEAD_DIM
SSM_HEADS = 16
SSM_HEAD_DIM = 64
SSM_INNER = SSM_HEADS * SSM_HEAD_DIM
SSM_GROUPS = 2
SSM_STATE = 128
SSM_CONV = 4
SSM_CHUNK = 128
BC_DIM = SSM_GROUPS * SSM_STATE
XBC_DIM = SSM_INNER + 2 * BC_DIM
D_FF = 2816
FFN_CONV = 3
PLE_DIM = 256
EPS = 1e-6

LANES = 128
SUBLANES = 8
QK_DIM = ATTN_DIM + KV_DIM
QKV_DIM = QK_DIM + KV_DIM
QKV_SLABS = QKV_DIM // LANES
ATTN_SLABS = ATTN_DIM // LANES
KV_PAIRS = ATTN_KV_HEADS // 2
VMEM_LIMIT = 56 * 1024 * 1024

MAX_DIL = 16
SUPER = MAX_DIL * ATTN_BLOCK
IN_TM = 512
TAIL_TM = 512
ROW_CLASSES = 8
FF_CHUNK = 256

assert DILATED_PATTERNS == ((128, 1), (512, 4), (2048, 16))
assert all(w // d == ATTN_BLOCK for w, d in DILATED_PATTERNS)


def _dot(a, b):
    return jnp.dot(a, b, preferred_element_type=F32)


def _dot_nt(a, b):
    return lax.dot_general(a, b, (((1,), (1,)), ((), ())), preferred_element_type=F32)


def _dot_tn(a, b):
    return lax.dot_general(a, b, (((0,), (0,)), ((), ())), preferred_element_type=F32)


def _rms(x):
    return x * lax.rsqrt(jnp.mean(x * x, axis=-1, keepdims=True) + EPS)


def _silu(x):
    return x * jax.nn.sigmoid(x)


def _const_spec(shape):
    nd = len(shape)
    return pl.BlockSpec(shape, lambda *_: (0,) * nd, pipeline_mode=pl.Buffered(1))


def _in_proj_kernel(tiles_per_batch,
                    x_ref, g_ref, wqk_ref, wv_ref, wz_ref, wxbc_ref, wdt_ref, qkg_ref, bd_ref,
                    cw_ref, cb_ref, dtb_ref,
                    qkv_ref, z_ref, xbc_ref, dt_ref, cbuf, nat, mod4):
    i = pl.program_id(0)
    tm = x_ref.shape[0]
    hb = (_rms(x_ref[...]) * g_ref[...]).astype(BF16)

    qk = _dot(hb, wqk_ref[...])
    sq = qk * qk
    sq_hi = sq.astype(BF16)
    sq_lo = (sq - sq_hi.astype(F32)).astype(BF16)
    bd = bd_ref[...]
    for c in range(QK_DIM // 256):
        sl = slice(c * 256, (c + 1) * 256)
        ss = _dot(sq_hi[:, sl], bd) + _dot(sq_lo[:, sl], bd)
        qkn = qk[:, sl] * lax.rsqrt(ss * (1.0 / HEAD_DIM) + EPS) * qkg_ref[:, sl]
        nat[2 * c] = qkn[:, :LANES]
        nat[2 * c + 1] = qkn[:, LANES:]
    v = _dot(hb, wv_ref[...])
    nat[QKV_SLABS - 2] = v[:, :LANES]
    nat[QKV_SLABS - 1] = v[:, LANES:]

    quarter = tm // 4
    sixteenth = tm // MAX_DIL
    for s in range(QKV_SLABS):
        for a in range(4):
            mod4[s, a * quarter:(a + 1) * quarter, :] = nat[s, pl.ds(a, quarter, stride=4), :]
    k = i % (SUPER // tm)
    for s in range(QKV_SLABS):
        for a in range(4):
            for b in range(4):
                dst = pl.multiple_of((4 * b + a) * ATTN_BLOCK + sixteenth * k, sixteenth)
                qkv_ref[s, pl.ds(dst, sixteenth), :] = mod4[s, pl.ds(a * quarter + b, sixteenth, stride=4), :]

    z_ref[...] = _dot(hb, wz_ref[...]).astype(BF16)

    @pl.when(i % tiles_per_batch == 0)
    def _():
        cbuf[0:SUBLANES, :] = jnp.zeros((SUBLANES, XBC_DIM), F32)

    cbuf[SUBLANES:SUBLANES + tm, :] = _dot(hb, wxbc_ref[...])
    acc = cb_ref[...] + cw_ref[SSM_CONV - 1:SSM_CONV, :] * cbuf[SUBLANES:SUBLANES + tm, :]
    for kk in range(SSM_CONV - 1):
        off = SUBLANES - (SSM_CONV - 1) + kk
        acc = acc + cw_ref[kk:kk + 1, :] * cbuf[off:off + tm, :]
    xbc_ref[...] = _silu(acc)
    cbuf[0:SUBLANES, :] = cbuf[tm:tm + SUBLANES, :]

    dtr = _dot(hb, wdt_ref[...]) + dtb_ref[...]
    dt_ref[...] = jnp.maximum(dtr, 0.0) + jnp.log1p(jnp.exp(-jnp.abs(dtr)))


def _in_proj(x2d, g, wqk, wv, wz, wxbc, wdt, qkg, bd, cw, cb, dtb, seq):
    t = x2d.shape[0]
    tm = IN_TM
    assert SUPER % tm == 0 and tm % (4 * MAX_DIL) == 0 and seq % SUPER == 0
    grid = (t // tm,)
    row = lambda w: pl.BlockSpec((tm, w), lambda i: (i, 0))
    per_super = SUPER // tm
    return pl.pallas_call(
        functools.partial(_in_proj_kernel, seq // tm),
        grid=grid,
        in_specs=[row(D_MODEL), _const_spec(g.shape), _const_spec(wqk.shape), _const_spec(wv.shape),
                  _const_spec(wz.shape), _const_spec(wxbc.shape), _const_spec(wdt.shape),
                  _const_spec(qkg.shape), _const_spec(bd.shape), _const_spec(cw.shape),
                  _const_spec(cb.shape), _const_spec(dtb.shape)],
        out_specs=[pl.BlockSpec((QKV_SLABS, SUPER, LANES), lambda i: (0, i // per_super, 0)),
                   row(SSM_INNER), row(XBC_DIM), row(LANES)],
        out_shape=[jax.ShapeDtypeStruct((QKV_SLABS, t, LANES), F32), jax.ShapeDtypeStruct((t, SSM_INNER), BF16),
                   jax.ShapeDtypeStruct((t, XBC_DIM), F32), jax.ShapeDtypeStruct((t, LANES), F32)],
        scratch_shapes=[pltpu.VMEM((tm + SUBLANES, XBC_DIM), F32),
                        pltpu.VMEM((QKV_SLABS, tm, LANES), F32),
                        pltpu.VMEM((QKV_SLABS, tm, LANES), F32)],
        compiler_params=pltpu.CompilerParams(dimension_semantics=("arbitrary",),
                                             vmem_limit_bytes=VMEM_LIMIT),
        name="in_proj",
    )(x2d, g, wqk, wv, wz, wxbc, wdt, qkg, bd, cw, cb, dtb)


def _block_perm(dil):
    reps = MAX_DIL // dil
    per = ATTN_BLOCK // reps
    pos = np.arange(ATTN_BLOCK)
    return (pos % per) * reps + pos // per


def _attn_bias():
    out = []
    for _, dil in DILATED_PATTERNS[::-1]:
        perm = _block_perm(dil)
        qi = perm[:, None]
        kj = np.concatenate([perm, ATTN_BLOCK + perm])[None, :]
        delta = ATTN_BLOCK + qi - kj
        band = (delta >= 0) & (delta <= ATTN_BLOCK)
        for first in (False, True):
            ok = band & (kj >= ATTN_BLOCK) if first else band
            out.append(np.tile(np.where(ok, 0.0, -np.inf).astype(np.float32), (4, 1)))
    return np.stack(out)


def _attn_kernel(cur_ref, prev_ref, bias_ref, out_ref, acc_o, acc_m, acc_l):
    first = (pl.program_id(1) == 0).astype(jnp.int32)
    lane = lax.broadcasted_iota(jnp.int32, (ATTN_BLOCK, LANES), 1)
    lo_half = lane < HEAD_DIM
    ones = jnp.ones((2 * ATTN_BLOCK, LANES), BF16)
    k_slab, v_slab = ATTN_SLABS, ATTN_SLABS + KV_PAIRS

    def rows(ref, slab, pieces):
        parts = [ref[slab, pl.ds(st, sz), :] for st, sz in pieces]
        return parts[0] if len(parts) == 1 else jnp.concatenate(parts, axis=0)

    def unit(q_pieces, prev_pieces, prev_in_cur, bias_idx, mode, out_start=None):
        bias = bias_ref[bias_idx]
        for p in range(KV_PAIRS):
            if prev_in_cur:
                kp, vp = rows(cur_ref, k_slab + p, prev_pieces), rows(cur_ref, v_slab + p, prev_pieces)
            else:
                kp, vp = rows(prev_ref, p, prev_pieces), rows(prev_ref, KV_PAIRS + p, prev_pieces)
            k2 = jnp.concatenate([kp, rows(cur_ref, k_slab + p, q_pieces)], axis=0).astype(BF16)
            v2 = jnp.concatenate([vp, rows(cur_ref, v_slab + p, q_pieces)], axis=0).astype(BF16)
            v2 = jnp.concatenate([v2, ones], axis=1)
            qs = []
            for g in range(ATTN_GROUP):
                q = rows(cur_ref, g * KV_PAIRS + p, q_pieces)
                qs += [jnp.where(lo_half, q, 0.0), jnp.where(lo_half, 0.0, q)]
            q4 = jnp.concatenate(qs, axis=0).astype(BF16)
            s = _dot_nt(q4, k2) + bias
            m = jnp.max(s, axis=-1, keepdims=True)
            pr = jnp.exp(s - m).astype(BF16)
            ol = _dot(pr, v2)
            for g in range(ATTN_GROUP):
                r0, r1 = (2 * g) * ATTN_BLOCK, (2 * g + 1) * ATTN_BLOCK
                o = jnp.where(lo_half, ol[r0:r0 + ATTN_BLOCK, :LANES], ol[r1:r1 + ATTN_BLOCK, :LANES])
                l = jnp.where(lo_half, ol[r0:r0 + ATTN_BLOCK, LANES:], ol[r1:r1 + ATTN_BLOCK, LANES:])
                mm = jnp.where(lo_half, m[r0:r0 + ATTN_BLOCK], m[r1:r1 + ATTN_BLOCK])
                slab = g * KV_PAIRS + p
                if mode != "init":
                    m_old = rows(acc_m, slab, q_pieces)
                    m_new = jnp.maximum(m_old, mm)
                    w_old, w_new = jnp.exp(m_old - m_new), jnp.exp(mm - m_new)
                    o = rows(acc_o, slab, q_pieces) * w_old + o * w_new
                    l = rows(acc_l, slab, q_pieces) * w_old + l * w_new
                    mm = m_new
                if mode == "final":
                    res = o / l
                    for r in range(MAX_DIL):
                        out_ref[slab, pl.ds(out_start + r, SUBLANES, stride=MAX_DIL), :] = (
                            res[r * SUBLANES:(r + 1) * SUBLANES])
                else:
                    off = 0
                    for st, sz in q_pieces:
                        acc_o[slab, pl.ds(st, sz), :] = o[off:off + sz]
                        acc_m[slab, pl.ds(st, sz), :] = mm[off:off + sz]
                        acc_l[slab, pl.ds(st, sz), :] = l[off:off + sz]
                        off += sz

    def aligned(x, m):
        return x if isinstance(x, int) else pl.multiple_of(x, m)

    def d16_body(r, carry):
        st = pl.multiple_of(r * ATTN_BLOCK, ATTN_BLOCK)
        unit([(st, ATTN_BLOCK)], [(st, ATTN_BLOCK)], False, first, "init")
        return carry
    lax.fori_loop(0, MAX_DIL, d16_body, 0)

    def d4_pieces(c, row0):
        return [(aligned((4 * a + c) * ATTN_BLOCK + row0, 32), 32) for a in range(4)]

    def d4_body(c, carry):
        unit(d4_pieces(c, 0), d4_pieces(c, ATTN_BLOCK - 32), False, 2 + first, "merge")

        def inner(n, carry2):
            unit(d4_pieces(c, 32 * n), d4_pieces(c, 32 * (n - 1)), True, 2, "merge")
            return carry2
        return lax.fori_loop(1, 4, inner, carry)
    lax.fori_loop(0, 4, d4_body, 0)

    def d1_pieces(row0):
        return [(aligned(r * ATTN_BLOCK + row0, SUBLANES), SUBLANES) for r in range(MAX_DIL)]

    unit(d1_pieces(0), d1_pieces(ATTN_BLOCK - SUBLANES), False, 4 + first, "final", 0)

    def d1_body(n, carry):
        unit(d1_pieces(SUBLANES * n), d1_pieces(SUBLANES * (n - 1)), True, 4, "final", n * ATTN_BLOCK)
        return carry
    lax.fori_loop(1, SUPER // ATTN_BLOCK, d1_body, 0)


def _attention(qkv, bias, bsz, seq):
    t = bsz * seq
    nsb = seq // SUPER
    acc = pltpu.VMEM((ATTN_SLABS, SUPER, LANES), F32)
    return pl.pallas_call(
        _attn_kernel,
        grid=(bsz, nsb),
        in_specs=[pl.BlockSpec((QKV_SLABS, SUPER, LANES), lambda b, s: (0, b * nsb + s, 0)),
                  pl.BlockSpec((QKV_SLABS // 2, SUPER, LANES),
                               lambda b, s: (1, b * nsb + jnp.maximum(s - 1, 0), 0)),
                  _const_spec(bias.shape)],
        out_specs=pl.BlockSpec((ATTN_SLABS, SUPER, LANES), lambda b, s: (0, b * nsb + s, 0)),
        out_shape=jax.ShapeDtypeStruct((ATTN_SLABS, t, LANES), F32),
        scratch_shapes=[acc, acc, acc],
        compiler_params=pltpu.CompilerParams(dimension_semantics=("arbitrary", "arbitrary"),
                                             vmem_limit_bytes=VMEM_LIMIT),
        name="attention",
    )(qkv, qkv, bias)


def _ssd_kernel(xbc_ref, dt_ref, z_ref, alog_ref, expand_ref, dskip_ref, ng_ref, out_ref, state_ref, y_ref):
    c = pl.program_id(1)
    ln = SSM_CHUNK

    @pl.when(c == 0)
    def _():
        state_ref[...] = jnp.zeros_like(state_ref)

    lane = lax.broadcasted_iota(jnp.int32, (ln, LANES), 1)
    lo_half = lane < SSM_HEAD_DIM
    part = lane // SSM_HEADS
    row = lax.broadcasted_iota(jnp.int32, (ln, ln), 0)
    col = lax.broadcasted_iota(jnp.int32, (ln, ln), 1)
    causal = row >= col
    tril = causal.astype(BF16)

    def split3(v):
        hi = v.astype(BF16).astype(F32)
        mid = (v - hi).astype(BF16).astype(F32)
        lo = (v - hi) - mid
        return hi, mid, lo

    def pack3(v):
        hi, mid, lo = split3(v)
        return jnp.where(part == 0, hi, jnp.where(part == 1, mid, jnp.where(part == 2, lo, 0.0))).astype(BF16)

    xs = xbc_ref[:, 0:SSM_INNER]
    dt = dt_ref[...]
    a = dt * (-jnp.exp(alog_ref[...]))
    a3 = jnp.concatenate(split3(a), axis=0).astype(BF16)
    acum = _dot(jnp.concatenate([tril, tril, tril], axis=1), a3)
    acum_t = acum.T
    total = acum[ln - 1:ln, :]
    packed = jnp.concatenate([pack3(dt), pack3(jnp.exp(acum)), pack3(jnp.exp(total - acum))], axis=0)
    expanded = _dot(packed, expand_ref[...])
    dt_e, ea_e, de_e = expanded[0:ln], expanded[ln:2 * ln], expanded[2 * ln:3 * ln]
    xdt = xs * dt_e
    xde = (xdt * de_e).astype(BF16)

    gw = SSM_INNER // SSM_GROUPS
    for g in range(SSM_GROUPS):
        bg = xbc_ref[:, SSM_INNER + g * SSM_STATE:SSM_INNER + (g + 1) * SSM_STATE].astype(BF16)
        cg = xbc_ref[:, SSM_INNER + BC_DIM + g * SSM_STATE:SSM_INNER + BC_DIM + (g + 1) * SSM_STATE].astype(BF16)
        cb = _dot_nt(cg, bg)
        for j in range(SSM_HEADS // SSM_GROUPS // 2):
            h0 = g * (SSM_HEADS // SSM_GROUPS) + 2 * j
            ms = []
            for h in (h0, h0 + 1):
                seg = acum[:, h:h + 1] - acum_t[h:h + 1, :]
                ms.append(cb * jnp.exp(jnp.where(causal, seg, -jnp.inf)))
            mm = jnp.concatenate(ms, axis=1).astype(BF16)
            xp = xdt[:, h0 * SSM_HEAD_DIM:(h0 + 2) * SSM_HEAD_DIM]
            rhs = jnp.concatenate([jnp.where(lo_half, xp, 0.0), jnp.where(lo_half, 0.0, xp)],
                                  axis=0).astype(BF16)
            y_ref[:, h0 * SSM_HEAD_DIM:(h0 + 2) * SSM_HEAD_DIM] = _dot(mm, rhs)
        gs = slice(g * gw, (g + 1) * gw)
        st = state_ref[g]
        y_ref[:, gs] = y_ref[:, gs] + _dot(cg, st.astype(BF16)) * ea_e[:, gs]
        state_ref[g] = st * ea_e[ln - 1:ln, gs] + _dot_tn(bg, xde[:, gs])

    y = y_ref[...] + dskip_ref[...] * xs
    y = y * _silu(z_ref[...].astype(F32))
    out_ref[...] = (_rms(y) * ng_ref[...]).astype(BF16)


def _ssd(xbc, dt, z, alog, expand, dskip, ng, bsz, seq):
    t = bsz * seq
    nc = seq // SSM_CHUNK
    row = lambda w: pl.BlockSpec((SSM_CHUNK, w), lambda b, c: (b * nc + c, 0))
    return pl.pallas_call(
        _ssd_kernel,
        grid=(bsz, nc),
        in_specs=[row(XBC_DIM), row(LANES), row(SSM_INNER), _const_spec(alog.shape),
                  _const_spec(expand.shape), _const_spec(dskip.shape), _const_spec(ng.shape)],
        out_specs=row(SSM_INNER),
        out_shape=jax.ShapeDtypeStruct((t, SSM_INNER), BF16),
        scratch_shapes=[pltpu.VMEM((SSM_GROUPS, SSM_STATE, SSM_INNER // SSM_GROUPS), F32),
                        pltpu.VMEM((SSM_CHUNK, SSM_INNER), F32)],
        compiler_params=pltpu.CompilerParams(dimension_semantics=("arbitrary", "arbitrary"),
                                             vmem_limit_bytes=VMEM_LIMIT),
        name="ssd",
    )(xbc, dt, z, alog, expand, dskip, ng)


def _tail_kernel(tiles_per_batch,
                 x_ref, attn_ref, ssm_ref, p_ref, woa_ref, wos_ref, fg_ref, wup_ref, fcw_ref, fcb_ref, wdn_ref,
                 pg_ref, wpg_ref, wpp_ref, out_ref, slabs, act_ref, carry, sbuf):
    i = pl.program_id(0)
    tm = x_ref.shape[0]
    n8 = tm // ROW_CLASSES
    nslab = D_MODEL // LANES

    x1 = x_ref[...] + _dot(ssm_ref[...], wos_ref[...])
    for s in range(ATTN_SLABS):
        x1 = x1 + _dot(attn_ref[s].astype(BF16), woa_ref[s * LANES:(s + 1) * LANES, :])

    h2 = _rms(x1) * fg_ref[...]
    for s in range(nslab):
        slabs[s] = h2[:, s * LANES:(s + 1) * LANES]
    h2p = jnp.concatenate(
        [jnp.concatenate([slabs[s, pl.ds(q, n8, stride=ROW_CLASSES), :] for q in range(ROW_CLASSES)], axis=0)
         for s in range(nslab)], axis=1).astype(BF16)

    @pl.when(i % tiles_per_batch == 0)
    def _():
        carry[...] = jnp.zeros_like(carry)

    def conv(u, cs, slot):
        blk = [u[q * n8:(q + 1) * n8] for q in range(ROW_CLASSES)]
        shifted = []
        for k in range(FFN_CONV - 1):
            sb = sbuf.at[slot * (FFN_CONV - 1) + k]
            sb[0:SUBLANES, :] = carry[k, :, cs]
            sb[SUBLANES:SUBLANES + n8, :] = blk[ROW_CLASSES - (FFN_CONV - 1) + k]
            shifted.append(sb[SUBLANES - 1:SUBLANES - 1 + n8, :])
            carry[k, :, cs] = sb[n8:n8 + SUBLANES, :]
        prev = shifted + blk
        w = [fcw_ref[k:k + 1, cs] for k in range(FFN_CONV)]
        b = fcb_ref[:, cs]
        return jnp.concatenate([b + w[2] * prev[q + 2] + w[1] * prev[q + 1] + w[0] * prev[q]
                                for q in range(ROW_CLASSES)], axis=0)

    for c in range(D_FF // FF_CHUNK):
        gs = slice(c * FF_CHUNK, (c + 1) * FF_CHUNK)
        vs = slice(D_FF + c * FF_CHUNK, D_FF + (c + 1) * FF_CHUNK)
        gate = conv(_dot(h2p, wup_ref[:, gs]), gs, 0)
        val = conv(_dot(h2p, wup_ref[:, vs]), vs, 1)
        act_ref[:, gs] = (_silu(gate) * val).astype(BF16)
    ffn_p = _dot(act_ref[...], wdn_ref[...])

    for s in range(nslab):
        for q in range(ROW_CLASSES):
            slabs[s, pl.ds(q, n8, stride=ROW_CLASSES), :] = ffn_p[q * n8:(q + 1) * n8, s * LANES:(s + 1) * LANES]
    x2 = x1 + jnp.concatenate([slabs[s] for s in range(nslab)], axis=1)

    h3 = (_rms(x2) * pg_ref[...]).astype(BF16)
    gate = jax.nn.sigmoid(_dot(h3, wpg_ref[...]))
    out_ref[...] = x2 + gate * _dot(p_ref[...].astype(BF16), wpp_ref[...])


def _tail(x2d, attn, ssm, p2d, woa, wos, fg, wup, fcw, fcb, wdn, pg, wpg, wpp, seq):
    t = x2d.shape[0]
    tm = TAIL_TM
    assert seq % tm == 0 and tm % (ROW_CLASSES * SUBLANES) == 0 and D_FF % FF_CHUNK == 0
    row = lambda w: pl.BlockSpec((tm, w), lambda i: (i, 0))
    consts = [woa, wos, fg, wup, fcw, fcb, wdn, pg, wpg, wpp]
    return pl.pallas_call(
        functools.partial(_tail_kernel, seq // tm),
        grid=(t // tm,),
        in_specs=[row(D_MODEL), pl.BlockSpec((ATTN_SLABS, tm, LANES), lambda i: (0, i, 0)),
                  row(SSM_INNER), row(PLE_DIM)] + [_const_spec(w.shape) for w in consts],
        out_specs=row(D_MODEL),
        out_shape=jax.ShapeDtypeStruct((t, D_MODEL), F32),
        scratch_shapes=[pltpu.VMEM((D_MODEL // LANES, tm, LANES), F32),
                        pltpu.VMEM((tm, D_FF), BF16),
                        pltpu.VMEM((FFN_CONV - 1, SUBLANES, 2 * D_FF), F32),
                        pltpu.VMEM((2 * (FFN_CONV - 1), SUBLANES + tm // ROW_CLASSES, FF_CHUNK), F32)],
        compiler_params=pltpu.CompilerParams(dimension_semantics=("arbitrary",),
                                             vmem_limit_bytes=VMEM_LIMIT),
        name="tail",
    )(x2d, attn, ssm, p2d, *consts)


def _q_perm():
    idx = np.arange(ATTN_DIM).reshape(ATTN_KV_HEADS, ATTN_GROUP, HEAD_DIM)
    return np.transpose(idx, (1, 0, 2)).reshape(-1)


def _layer(x, p, attn_norm_g, w_in, q_norm_g, k_norm_g, ssm_conv_w, ssm_conv_b, dt_bias, a_log,
           d_skip, ssm_norm_g, w_out, ffn_norm_g, w_up, ffn_conv_w, ffn_conv_b, w_down,
           ple_norm_g, w_ple_gate, w_ple_proj):
    bsz, seq, _ = x.shape
    t = bsz * seq
    x2d = x.reshape(t, D_MODEL)
    p2d = p.reshape(t, PLE_DIM)
    qperm = _q_perm()

    o_k, o_v, o_z = ATTN_DIM, ATTN_DIM + KV_DIM, ATTN_DIM + 2 * KV_DIM
    o_xbc, o_dt = o_z + SSM_INNER, o_z + SSM_INNER + XBC_DIM
    wqk = jnp.concatenate([w_in[:, :ATTN_DIM][:, qperm], w_in[:, o_k:o_v]], axis=1).astype(BF16)
    wv = w_in[:, o_v:o_z].astype(BF16)
    wz = w_in[:, o_z:o_xbc].astype(BF16)
    wxbc = w_in[:, o_xbc:o_dt].astype(BF16)
    head_reps = LANES // SSM_HEADS
    wdt = jnp.tile(w_in[:, o_dt:], (1, head_reps)).astype(BF16)
    qkg = jnp.concatenate([jnp.tile(q_norm_g * (HEAD_DIM ** -0.5), ATTN_Q_HEADS),
                           jnp.tile(k_norm_g, ATTN_KV_HEADS)]).reshape(1, QK_DIM)
    hid = np.arange(256) // HEAD_DIM
    bd = jnp.asarray(hid[:, None] == hid[None, :], BF16)
    dtb = jnp.tile(dt_bias, head_reps).reshape(1, LANES)
    alog = jnp.tile(a_log, head_reps).reshape(1, LANES)

    qkv, z, xbc, dt = _in_proj(x2d, attn_norm_g.reshape(1, D_MODEL), wqk, wv, wz, wxbc, wdt, qkg, bd,
                               ssm_conv_w, ssm_conv_b.reshape(1, XBC_DIM), dtb, seq)

    attn = _attention(qkv, jnp.asarray(_attn_bias()), bsz, seq)

    src_lane = np.arange(LANES)[:, None]
    expand = jnp.asarray((src_lane % SSM_HEADS == (np.arange(SSM_INNER) // SSM_HEAD_DIM)[None, :])
                         & (src_lane < 3 * SSM_HEADS), BF16)
    dskip = jnp.repeat(d_skip, SSM_HEAD_DIM).reshape(1, SSM_INNER)
    ssm = _ssd(xbc, dt, z, alog, expand, dskip, ssm_norm_g.reshape(1, SSM_INNER), bsz, seq)

    woa = w_out[:ATTN_DIM][qperm].astype(BF16)
    wos = w_out[ATTN_DIM:].astype(BF16)
    out = _tail(x2d, attn, ssm, p2d, woa, wos, ffn_norm_g.reshape(1, D_MODEL), w_up.astype(BF16),
                ffn_conv_w, ffn_conv_b.reshape(1, 2 * D_FF), w_down.astype(BF16),
                ple_norm_g.reshape(1, D_MODEL), w_ple_gate.astype(BF16), w_ple_proj.astype(BF16), seq)
    return out.reshape(bsz, seq, D_MODEL)


def kernel(x, p, attn_norm_g, w_in, q_norm_g, k_norm_g, ssm_conv_w, ssm_conv_b, dt_bias, a_log, d_skip,
           ssm_norm_g, w_out, ffn_norm_g, w_up, ffn_conv_w, ffn_conv_b, w_down, ple_norm_g, w_ple_gate,
           w_ple_proj):
    depth = w_in.shape[0]
    for i in range(depth):
        x = _layer(x, p[i], attn_norm_g[i], w_in[i], q_norm_g[i], k_norm_g[i], ssm_conv_w[i], ssm_conv_b[i],
                   dt_bias[i], a_log[i], d_skip[i], ssm_norm_g[i], w_out[i], ffn_norm_g[i], w_up[i],
                   ffn_conv_w[i], ffn_conv_b[i], w_down[i], ple_norm_g[i], w_ple_gate[i], w_ple_proj[i])
    return x
```

```python
import functools

import jax
import jax.numpy as jnp
import numpy as np
from jax import lax
from jax.experimental import pallas as pl
from jax.experimental.pallas import tpu as pltpu

F32 = jnp.float32
BF16 = jnp.bfloat16

D_MODEL = 1024
HEAD_DIM = 64
ATTN_Q_HEADS = 8
ATTN_KV_HEADS = 4
ATTN_GROUP = ATTN_Q_HEADS // ATTN_KV_HEADS
DILATED_PATTERNS = ((128, 1), (512, 4), (2048, 16))
ATTN_BLOCK = 128
ATTN_DIM = ATTN_Q_HEADS * HEAD_DIM
KV_DIM = ATTN_KV_HEADS * HEAD_DIM
SSM_HEADS = 16
SSM_HEAD_DIM = 64
SSM_INNER = SSM_HEADS * SSM_HEAD_DIM
SSM_GROUPS = 2
SSM_STATE = 128
SSM_CONV = 4
SSM_CHUNK = 128
BC_DIM = SSM_GROUPS * SSM_STATE
XBC_DIM = SSM_INNER + 2 * BC_DIM
D_FF = 2816
FFN_CONV = 3
PLE_DIM = 256
EPS = 1e-6
LOG2E = 1.4426950408889634

LANES = 128
SUBLANES = 8
QK_DIM = ATTN_DIM + KV_DIM
QKV_DIM = QK_DIM + KV_DIM
QKV_SLABS = QKV_DIM // LANES
ATTN_SLABS = ATTN_DIM // LANES
KV_PAIRS = ATTN_KV_HEADS // 2
VMEM_LIMIT = 56 * 1024 * 1024

MAX_DIL = 16
SUPER = MAX_DIL * ATTN_BLOCK
IN_TM = 512
TAIL_TM = 512
ROW_CLASSES = 8
FF_CHUNK = 256

assert DILATED_PATTERNS == ((128, 1), (512, 4), (2048, 16))
assert all(w // d == ATTN_BLOCK for w, d in DILATED_PATTERNS)


def _dot(a, b):
    return jnp.dot(a, b, preferred_element_type=F32)


def _dot_nt(a, b):
    return lax.dot_general(a, b, (((1,), (1,)), ((), ())), preferred_element_type=F32)


def _dot_tn(a, b):
    return lax.dot_general(a, b, (((0,), (0,)), ((), ())), preferred_element_type=F32)


def _rms(x):
    return x * lax.rsqrt(jnp.mean(x * x, axis=-1, keepdims=True) + EPS)


def _silu(x):
    return x * jax.nn.sigmoid(x)


def _const_spec(shape):
    nd = len(shape)
    return pl.BlockSpec(shape, lambda *_: (0,) * nd, pipeline_mode=pl.Buffered(1))


def _in_proj_kernel(tiles_per_batch,
                    x_ref, g_ref, wqk_ref, wv_ref, wz_ref, wxbc_ref, wdt_ref, qkg_ref, bd_ref,
                    cw_ref, cb_ref, dtb_ref,
                    qkv_ref, z_ref, xbc_ref, dt_ref, cbuf, nat, mod4):
    i = pl.program_id(0)
    tm = x_ref.shape[0]
    hb = (_rms(x_ref[...]) * g_ref[...]).astype(BF16)

    qk = _dot(hb, wqk_ref[...])
    sq = qk * qk
    sq_hi = sq.astype(BF16)
    sq_lo = (sq - sq_hi.astype(F32)).astype(BF16)
    bd = bd_ref[...]
    for c in range(QK_DIM // 256):
        sl = slice(c * 256, (c + 1) * 256)
        ss = _dot(sq_hi[:, sl], bd) + _dot(sq_lo[:, sl], bd)
        qkn = qk[:, sl] * lax.rsqrt(ss * (1.0 / HEAD_DIM) + EPS) * qkg_ref[:, sl]
        nat[2 * c] = qkn[:, :LANES]
        nat[2 * c + 1] = qkn[:, LANES:]
    v = _dot(hb, wv_ref[...])
    nat[QKV_SLABS - 2] = v[:, :LANES]
    nat[QKV_SLABS - 1] = v[:, LANES:]

    quarter = tm // 4
    sixteenth = tm // MAX_DIL
    for s in range(QKV_SLABS):
        for a in range(4):
            mod4[s, a * quarter:(a + 1) * quarter, :] = nat[s, pl.ds(a, quarter, stride=4), :]
    k = i % (SUPER // tm)
    for s in range(QKV_SLABS):
        for a in range(4):
            for b in range(4):
                dst = pl.multiple_of((4 * b + a) * ATTN_BLOCK + sixteenth * k, sixteenth)
                qkv_ref[s, pl.ds(dst, sixteenth), :] = mod4[s, pl.ds(a * quarter + b, sixteenth, stride=4), :]

    z_ref[...] = _dot(hb, wz_ref[...]).astype(BF16)

    @pl.when(i % tiles_per_batch == 0)
    def _():
        cbuf[0:SUBLANES, :] = jnp.zeros((SUBLANES, XBC_DIM), F32)

    cbuf[SUBLANES:SUBLANES + tm, :] = _dot(hb, wxbc_ref[...])
    acc = cb_ref[...] + cw_ref[SSM_CONV - 1:SSM_CONV, :] * cbuf[SUBLANES:SUBLANES + tm, :]
    for kk in range(SSM_CONV - 1):
        off = SUBLANES - (SSM_CONV - 1) + kk
        acc = acc + cw_ref[kk:kk + 1, :] * cbuf[off:off + tm, :]
    xbc_ref[...] = _silu(acc)
    cbuf[0:SUBLANES, :] = cbuf[tm:tm + SUBLANES, :]

    dtr = _dot(hb, wdt_ref[...]) + dtb_ref[...]
    dt_ref[...] = jnp.maximum(dtr, 0.0) + jnp.log1p(jnp.exp(-jnp.abs(dtr)))


def _in_proj(x2d, g, wqk, wv, wz, wxbc, wdt, qkg, bd, cw, cb, dtb, seq):
    t = x2d.shape[0]
    tm = IN_TM
    assert SUPER % tm == 0 and tm % (4 * MAX_DIL) == 0 and seq % SUPER == 0
    grid = (t // tm,)
    row = lambda w: pl.BlockSpec((tm, w), lambda i: (i, 0))
    per_super = SUPER // tm
    return pl.pallas_call(
        functools.partial(_in_proj_kernel, seq // tm),
        grid=grid,
        in_specs=[row(D_MODEL), _const_spec(g.shape), _const_spec(wqk.shape), _const_spec(wv.shape),
                  _const_spec(wz.shape), _const_spec(wxbc.shape), _const_spec(wdt.shape),
                  _const_spec(qkg.shape), _const_spec(bd.shape), _const_spec(cw.shape),
                  _const_spec(cb.shape), _const_spec(dtb.shape)],
        out_specs=[pl.BlockSpec((QKV_SLABS, SUPER, LANES), lambda i: (0, i // per_super, 0)),
                   row(SSM_INNER), row(XBC_DIM), row(LANES)],
        out_shape=[jax.ShapeDtypeStruct((QKV_SLABS, t, LANES), F32), jax.ShapeDtypeStruct((t, SSM_INNER), BF16),
                   jax.ShapeDtypeStruct((t, XBC_DIM), F32), jax.ShapeDtypeStruct((t, LANES), F32)],
        scratch_shapes=[pltpu.VMEM((tm + SUBLANES, XBC_DIM), F32),
                        pltpu.VMEM((QKV_SLABS, tm, LANES), F32),
                        pltpu.VMEM((QKV_SLABS, tm, LANES), F32)],
        compiler_params=pltpu.CompilerParams(dimension_semantics=("arbitrary",),
                                             vmem_limit_bytes=VMEM_LIMIT),
        name="in_proj",
    )(x2d, g, wqk, wv, wz, wxbc, wdt, qkg, bd, cw, cb, dtb)


def _block_perm(dil):
    reps = MAX_DIL // dil
    per = ATTN_BLOCK // reps
    pos = np.arange(ATTN_BLOCK)
    return (pos % per) * reps + pos // per


def _attn_bias():
    out = []
    for _, dil in DILATED_PATTERNS[::-1]:
        perm = _block_perm(dil)
        qi = perm[:, None]
        kj = np.concatenate([perm, ATTN_BLOCK + perm])[None, :]
        delta = ATTN_BLOCK + qi - kj
        band = (delta >= 0) & (delta <= ATTN_BLOCK)
        for first in (False, True):
            ok = band & (kj >= ATTN_BLOCK) if first else band
            out.append(np.tile(np.where(ok, 0.0, -np.inf).astype(np.float32), (4, 1)))
    return np.stack(out)


def _attn_kernel(cur_ref, prev_ref, bias_ref, out_ref, acc_o, acc_m, acc_l):
    first = (pl.program_id(1) == 0).astype(jnp.int32)
    lane = lax.broadcasted_iota(jnp.int32, (ATTN_BLOCK, LANES), 1)
    lo_half = lane < HEAD_DIM
    ones = jnp.ones((2 * ATTN_BLOCK, LANES), BF16)
    k_slab, v_slab = ATTN_SLABS, ATTN_SLABS + KV_PAIRS

    def rows(ref, slab, pieces):
        parts = [ref[slab, pl.ds(st, sz), :] for st, sz in pieces]
        return parts[0] if len(parts) == 1 else jnp.concatenate(parts, axis=0)

    def unit(q_pieces, prev_pieces, prev_in_cur, bias_idx, mode, out_start=None):
        bias = bias_ref[bias_idx]
        for p in range(KV_PAIRS):
            if prev_in_cur:
                kp, vp = rows(cur_ref, k_slab + p, prev_pieces), rows(cur_ref, v_slab + p, prev_pieces)
            else:
                kp, vp = rows(prev_ref, p, prev_pieces), rows(prev_ref, KV_PAIRS + p, prev_pieces)
            k2 = jnp.concatenate([kp, rows(cur_ref, k_slab + p, q_pieces)], axis=0).astype(BF16)
            v2 = jnp.concatenate([vp, rows(cur_ref, v_slab + p, q_pieces)], axis=0).astype(BF16)
            v2 = jnp.concatenate([v2, ones], axis=1)
            qs = []
            for g in range(ATTN_GROUP):
                q = rows(cur_ref, g * KV_PAIRS + p, q_pieces)
                qs += [jnp.where(lo_half, q, 0.0), jnp.where(lo_half, 0.0, q)]
            q4 = jnp.concatenate(qs, axis=0).astype(BF16)
            s = _dot_nt(q4, k2) + bias
            m = jnp.max(s, axis=-1, keepdims=True)
            pr = jnp.exp2(s - m).astype(BF16)
            ol = _dot(pr, v2)
            for g in range(ATTN_GROUP):
                r0, r1 = (2 * g) * ATTN_BLOCK, (2 * g + 1) * ATTN_BLOCK
                o = jnp.where(lo_half, ol[r0:r0 + ATTN_BLOCK, :LANES], ol[r1:r1 + ATTN_BLOCK, :LANES])
                l = jnp.where(lo_half, ol[r0:r0 + ATTN_BLOCK, LANES:], ol[r1:r1 + ATTN_BLOCK, LANES:])
                mm = jnp.where(lo_half, m[r0:r0 + ATTN_BLOCK], m[r1:r1 + ATTN_BLOCK])
                slab = g * KV_PAIRS + p
                if mode != "init":
                    m_old = rows(acc_m, slab, q_pieces)
                    m_new = jnp.maximum(m_old, mm)
                    w_old, w_new = jnp.exp2(m_old - m_new), jnp.exp2(mm - m_new)
                    o = rows(acc_o, slab, q_pieces) * w_old + o * w_new
                    l = rows(acc_l, slab, q_pieces) * w_old + l * w_new
                    mm = m_new
                if mode == "final":
                    res = o / l
                    for r in range(MAX_DIL):
                        out_ref[slab, pl.ds(out_start + r, SUBLANES, stride=MAX_DIL), :] = (
                            res[r * SUBLANES:(r + 1) * SUBLANES])
                else:
                    off = 0
                    for st, sz in q_pieces:
                        acc_o[slab, pl.ds(st, sz), :] = o[off:off + sz]
                        acc_m[slab, pl.ds(st, sz), :] = mm[off:off + sz]
                        acc_l[slab, pl.ds(st, sz), :] = l[off:off + sz]
                        off += sz

    def aligned(x, m):
        return x if isinstance(x, int) else pl.multiple_of(x, m)

    def d16_body(r, carry):
        st = pl.multiple_of(r * ATTN_BLOCK, ATTN_BLOCK)
        unit([(st, ATTN_BLOCK)], [(st, ATTN_BLOCK)], False, first, "init")
        return carry
    lax.fori_loop(0, MAX_DIL, d16_body, 0, unroll=4)

    def d4_pieces(c, row0):
        return [(aligned((4 * a + c) * ATTN_BLOCK + row0, 32), 32) for a in range(4)]

    def d4_body(c, carry):
        unit(d4_pieces(c, 0), d4_pieces(c, ATTN_BLOCK - 32), False, 2 + first, "merge")

        for n in range(1, 4):
            unit(d4_pieces(c, 32 * n), d4_pieces(c, 32 * (n - 1)), True, 2, "merge")
        return carry
    lax.fori_loop(0, 4, d4_body, 0)

    def d1_pieces(row0):
        return [(aligned(r * ATTN_BLOCK + row0, SUBLANES), SUBLANES) for r in range(MAX_DIL)]

    unit(d1_pieces(0), d1_pieces(ATTN_BLOCK - SUBLANES), False, 4 + first, "final", 0)

    def d1_body(n, carry):
        unit(d1_pieces(SUBLANES * n), d1_pieces(SUBLANES * (n - 1)), True, 4, "final", n * ATTN_BLOCK)
        return carry
    lax.fori_loop(1, SUPER // ATTN_BLOCK, d1_body, 0, unroll=3)


def _attention(qkv, bias, bsz, seq):
    t = bsz * seq
    nsb = seq // SUPER
    acc = pltpu.VMEM((ATTN_SLABS, SUPER, LANES), F32)
    return pl.pallas_call(
        _attn_kernel,
        grid=(bsz, nsb),
        in_specs=[pl.BlockSpec((QKV_SLABS, SUPER, LANES), lambda b, s: (0, b * nsb + s, 0)),
                  pl.BlockSpec((QKV_SLABS // 2, SUPER, LANES),
                               lambda b, s: (1, b * nsb + jnp.maximum(s - 1, 0), 0)),
                  _const_spec(bias.shape)],
        out_specs=pl.BlockSpec((ATTN_SLABS, SUPER, LANES), lambda b, s: (0, b * nsb + s, 0)),
        out_shape=jax.ShapeDtypeStruct((ATTN_SLABS, t, LANES), F32),
        scratch_shapes=[acc, acc, acc],
        compiler_params=pltpu.CompilerParams(dimension_semantics=("arbitrary", "arbitrary"),
                                             vmem_limit_bytes=VMEM_LIMIT),
        name="attention",
    )(qkv, qkv, bias)


def _ssd_kernel(xbc_ref, dt_ref, z_ref, alog_ref, expand_ref, dskip_ref, ng_ref, out_ref, state_ref, y_ref):
    c = pl.program_id(1)
    ln = SSM_CHUNK

    @pl.when(c == 0)
    def _():
        state_ref[...] = jnp.zeros_like(state_ref)

    lane = lax.broadcasted_iota(jnp.int32, (ln, LANES), 1)
    lo_half = lane < SSM_HEAD_DIM
    part = lane // SSM_HEADS
    row = lax.broadcasted_iota(jnp.int32, (ln, ln), 0)
    col = lax.broadcasted_iota(jnp.int32, (ln, ln), 1)
    causal = row >= col
    tril = causal.astype(BF16)

    def split3(v):
        hi = v.astype(BF16).astype(F32)
        mid = (v - hi).astype(BF16).astype(F32)
        lo = (v - hi) - mid
        return hi, mid, lo

    def pack3(v):
        hi, mid, lo = split3(v)
        return jnp.where(part == 0, hi, jnp.where(part == 1, mid, jnp.where(part == 2, lo, 0.0))).astype(BF16)

    xs = xbc_ref[:, 0:SSM_INNER]
    dt = dt_ref[...]
    a = dt * (-jnp.exp(alog_ref[...]))
    a3 = jnp.concatenate(split3(a), axis=0).astype(BF16)
    acum = _dot(jnp.concatenate([tril, tril, tril], axis=1), a3)
    acum_t = acum.T
    total = acum[ln - 1:ln, :]
    packed = jnp.concatenate([pack3(dt), pack3(jnp.exp(acum)), pack3(jnp.exp(total - acum))], axis=0)
    expanded = _dot(packed, expand_ref[...])
    dt_e, ea_e, de_e = expanded[0:ln], expanded[ln:2 * ln], expanded[2 * ln:3 * ln]
    xdt = xs * dt_e
    xde = (xdt * de_e).astype(BF16)

    gw = SSM_INNER // SSM_GROUPS
    for g in range(SSM_GROUPS):
        bg = xbc_ref[:, SSM_INNER + g * SSM_STATE:SSM_INNER + (g + 1) * SSM_STATE].astype(BF16)
        cg = xbc_ref[:, SSM_INNER + BC_DIM + g * SSM_STATE:SSM_INNER + BC_DIM + (g + 1) * SSM_STATE].astype(BF16)
        cb = _dot_nt(cg, bg)
        for j in range(SSM_HEADS // SSM_GROUPS // 2):
            h0 = g * (SSM_HEADS // SSM_GROUPS) + 2 * j
            ms = []
            for h in (h0, h0 + 1):
                seg = acum[:, h:h + 1] - acum_t[h:h + 1, :]
                ms.append(cb * jnp.exp(jnp.where(causal, seg, -jnp.inf)))
            mm = jnp.concatenate(ms, axis=1).astype(BF16)
            xp = xdt[:, h0 * SSM_HEAD_DIM:(h0 + 2) * SSM_HEAD_DIM]
            rhs = jnp.concatenate([jnp.where(lo_half, xp, 0.0), jnp.where(lo_half, 0.0, xp)],
                                  axis=0).astype(BF16)
            y_ref[:, h0 * SSM_HEAD_DIM:(h0 + 2) * SSM_HEAD_DIM] = _dot(mm, rhs)
        gs = slice(g * gw, (g + 1) * gw)
        st = state_ref[g]
        y_ref[:, gs] = y_ref[:, gs] + _dot(cg, st.astype(BF16)) * ea_e[:, gs]
        state_ref[g] = st * ea_e[ln - 1:ln, gs] + _dot_tn(bg, xde[:, gs])

    y = y_ref[...] + dskip_ref[...] * xs
    y = y * _silu(z_ref[...].astype(F32))
    out_ref[...] = (_rms(y) * ng_ref[...]).astype(BF16)


def _ssd(xbc, dt, z, alog, expand, dskip, ng, bsz, seq):
    t = bsz * seq
    nc = seq // SSM_CHUNK
    row = lambda w: pl.BlockSpec((SSM_CHUNK, w), lambda b, c: (b * nc + c, 0))
    return pl.pallas_call(
        _ssd_kernel,
        grid=(bsz, nc),
        in_specs=[row(XBC_DIM), row(LANES), row(SSM_INNER), _const_spec(alog.shape),
                  _const_spec(expand.shape), _const_spec(dskip.shape), _const_spec(ng.shape)],
        out_specs=row(SSM_INNER),
        out_shape=jax.ShapeDtypeStruct((t, SSM_INNER), BF16),
        scratch_shapes=[pltpu.VMEM((SSM_GROUPS, SSM_STATE, SSM_INNER // SSM_GROUPS), F32),
                        pltpu.VMEM((SSM_CHUNK, SSM_INNER), F32)],
        compiler_params=pltpu.CompilerParams(dimension_semantics=("arbitrary", "arbitrary"),
                                             vmem_limit_bytes=VMEM_LIMIT),
        name="ssd",
    )(xbc, dt, z, alog, expand, dskip, ng)


def _tail_kernel(tiles_per_batch,
                 x_ref, attn_ref, ssm_ref, p_ref, woa_ref, wos_ref, fg_ref, wup_ref, fcw_ref, fcb_ref, wdn_ref,
                 pg_ref, wpg_ref, wpp_ref, out_ref, slabs, act_ref, carry, sbuf):
    i = pl.program_id(0)
    tm = x_ref.shape[0]
    n8 = tm // ROW_CLASSES
    nslab = D_MODEL // LANES

    x1 = x_ref[...] + _dot(ssm_ref[...], wos_ref[...])
    for s in range(ATTN_SLABS):
        x1 = x1 + _dot(attn_ref[s].astype(BF16), woa_ref[s * LANES:(s + 1) * LANES, :])

    h2 = _rms(x1) * fg_ref[...]
    for s in range(nslab):
        slabs[s] = h2[:, s * LANES:(s + 1) * LANES]
    h2p = jnp.concatenate(
        [jnp.concatenate([slabs[s, pl.ds(q, n8, stride=ROW_CLASSES), :] for q in range(ROW_CLASSES)], axis=0)
         for s in range(nslab)], axis=1).astype(BF16)

    @pl.when(i % tiles_per_batch == 0)
    def _():
        carry[...] = jnp.zeros_like(carry)

    def conv(u, cs, slot):
        blk = [u[q * n8:(q + 1) * n8] for q in range(ROW_CLASSES)]
        shifted = []
        for k in range(FFN_CONV - 1):
            sb = sbuf.at[slot * (FFN_CONV - 1) + k]
            sb[0:SUBLANES, :] = carry[k, :, cs]
            sb[SUBLANES:SUBLANES + n8, :] = blk[ROW_CLASSES - (FFN_CONV - 1) + k]
            shifted.append(sb[SUBLANES - 1:SUBLANES - 1 + n8, :])
            carry[k, :, cs] = sb[n8:n8 + SUBLANES, :]
        prev = shifted + blk
        w = [fcw_ref[k:k + 1, cs] for k in range(FFN_CONV)]
        b = fcb_ref[:, cs]
        return jnp.concatenate([b + w[2] * prev[q + 2] + w[1] * prev[q + 1] + w[0] * prev[q]
                                for q in range(ROW_CLASSES)], axis=0)

    for c in range(D_FF // FF_CHUNK):
        gs = slice(c * FF_CHUNK, (c + 1) * FF_CHUNK)
        vs = slice(D_FF + c * FF_CHUNK, D_FF + (c + 1) * FF_CHUNK)
        gate = conv(_dot(h2p, wup_ref[:, gs]), gs, 0)
        val = conv(_dot(h2p, wup_ref[:, vs]), vs, 1)
        act_ref[:, gs] = (_silu(gate) * val).astype(BF16)
    ffn_p = _dot(act_ref[...], wdn_ref[...])

    for s in range(nslab):
        for q in range(ROW_CLASSES):
            slabs[s, pl.ds(q, n8, stride=ROW_CLASSES), :] = ffn_p[q * n8:(q + 1) * n8, s * LANES:(s + 1) * LANES]
    x2 = x1 + jnp.concatenate([slabs[s] for s in range(nslab)], axis=1)

    h3 = (_rms(x2) * pg_ref[...]).astype(BF16)
    gate = jax.nn.sigmoid(_dot(h3, wpg_ref[...]))
    out_ref[...] = x2 + gate * _dot(p_ref[...].astype(BF16), wpp_ref[...])


def _tail(x2d, attn, ssm, p2d, woa, wos, fg, wup, fcw, fcb, wdn, pg, wpg, wpp, seq):
    t = x2d.shape[0]
    tm = TAIL_TM
    assert seq % tm == 0 and tm % (ROW_CLASSES * SUBLANES) == 0 and D_FF % FF_CHUNK == 0
    row = lambda w: pl.BlockSpec((tm, w), lambda i: (i, 0))
    consts = [woa, wos, fg, wup, fcw, fcb, wdn, pg, wpg, wpp]
    return pl.pallas_call(
        functools.partial(_tail_kernel, seq // tm),
        grid=(t // tm,),
        in_specs=[row(D_MODEL), pl.BlockSpec((ATTN_SLABS, tm, LANES), lambda i: (0, i, 0)),
                  row(SSM_INNER), row(PLE_DIM)] + [_const_spec(w.shape) for w in consts],
        out_specs=row(D_MODEL),
        out_shape=jax.ShapeDtypeStruct((t, D_MODEL), F32),
        scratch_shapes=[pltpu.VMEM((D_MODEL // LANES, tm, LANES), F32),
                        pltpu.VMEM((tm, D_FF), BF16),
                        pltpu.VMEM((FFN_CONV - 1, SUBLANES, 2 * D_FF), F32),
                        pltpu.VMEM((2 * (FFN_CONV - 1), SUBLANES + tm // ROW_CLASSES, FF_CHUNK), F32)],
        compiler_params=pltpu.CompilerParams(dimension_semantics=("arbitrary",),
                                             vmem_limit_bytes=VMEM_LIMIT),
        name="tail",
    )(x2d, attn, ssm, p2d, *consts)


def _q_perm():
    idx = np.arange(ATTN_DIM).reshape(ATTN_KV_HEADS, ATTN_GROUP, HEAD_DIM)
    return np.transpose(idx, (1, 0, 2)).reshape(-1)


def _layer(x, p, attn_norm_g, w_in, q_norm_g, k_norm_g, ssm_conv_w, ssm_conv_b, dt_bias, a_log,
           d_skip, ssm_norm_g, w_out, ffn_norm_g, w_up, ffn_conv_w, ffn_conv_b, w_down,
           ple_norm_g, w_ple_gate, w_ple_proj):
    bsz, seq, _ = x.shape
    t = bsz * seq
    x2d = x.reshape(t, D_MODEL)
    p2d = p.reshape(t, PLE_DIM)
    qperm = _q_perm()

    o_k, o_v, o_z = ATTN_DIM, ATTN_DIM + KV_DIM, ATTN_DIM + 2 * KV_DIM
    o_xbc, o_dt = o_z + SSM_INNER, o_z + SSM_INNER + XBC_DIM
    wqk = jnp.concatenate([w_in[:, :ATTN_DIM][:, qperm], w_in[:, o_k:o_v]], axis=1).astype(BF16)
    wv = w_in[:, o_v:o_z].astype(BF16)
    wz = w_in[:, o_z:o_xbc].astype(BF16)
    wxbc = w_in[:, o_xbc:o_dt].astype(BF16)
    head_reps = LANES // SSM_HEADS
    wdt = jnp.tile(w_in[:, o_dt:], (1, head_reps)).astype(BF16)
    qkg = jnp.concatenate([jnp.tile(q_norm_g * (HEAD_DIM ** -0.5 * LOG2E), ATTN_Q_HEADS),
                           jnp.tile(k_norm_g, ATTN_KV_HEADS)]).reshape(1, QK_DIM)
    hid = np.arange(256) // HEAD_DIM
    bd = jnp.asarray(hid[:, None] == hid[None, :], BF16)
    dtb = jnp.tile(dt_bias, head_reps).reshape(1, LANES)
    alog = jnp.tile(a_log, head_reps).reshape(1, LANES)

    qkv, z, xbc, dt = _in_proj(x2d, attn_norm_g.reshape(1, D_MODEL), wqk, wv, wz, wxbc, wdt, qkg, bd,
                               ssm_conv_w, ssm_conv_b.reshape(1, XBC_DIM), dtb, seq)

    attn = _attention(qkv, jnp.asarray(_attn_bias()), bsz, seq)

    src_lane = np.arange(LANES)[:, None]
    expand = jnp.asarray((src_lane % SSM_HEADS == (np.arange(SSM_INNER) // SSM_HEAD_DIM)[None, :])
                         & (src_lane < 3 * SSM_HEADS), BF16)
    dskip = jnp.repeat(d_skip, SSM_HEAD_DIM).reshape(1, SSM_INNER)
    ssm = _ssd(xbc, dt, z, alog, expand, dskip, ssm_norm_g.reshape(1, SSM_INNER), bsz, seq)

    woa = w_out[:ATTN_DIM][qperm].astype(BF16)
    wos = w_out[ATTN_DIM:].astype(BF16)
    out = _tail(x2d, attn, ssm, p2d, woa, wos, ffn_norm_g.reshape(1, D_MODEL), w_up.astype(BF16),
                ffn_conv_w, ffn_conv_b.reshape(1, 2 * D_FF), w_down.astype(BF16),
                ple_norm_g.reshape(1, D_MODEL), w_ple_gate.astype(BF16), w_ple_proj.astype(BF16), seq)
    return out.reshape(bsz, seq, D_MODEL)


def kernel(x, p, attn_norm_g, w_in, q_norm_g, k_norm_g, ssm_conv_w, ssm_conv_b, dt_bias, a_log, d_skip,
           ssm_norm_g, w_out, ffn_norm_g, w_up, ffn_conv_w, ffn_conv_b, w_down, ple_norm_g, w_ple_gate,
           w_ple_proj):
    depth = w_in.shape[0]
    for i in range(depth):
        x = _layer(x, p[i], attn_norm_g[i], w_in[i], q_norm_g[i], k_norm_g[i], ssm_conv_w[i], ssm_conv_b[i],
                   dt_bias[i], a_log[i], d_skip[i], ssm_norm_g[i], w_out[i], ffn_norm_g[i], w_up[i],
                   ffn_conv_w[i], ffn_conv_b[i], w_down[i], ple_norm_g[i], w_ple_gate[i], w_ple_proj[i])
    return x
```

```python
import functools

import jax
import jax.numpy as jnp
import numpy as np
from jax import lax
from jax.experimental import pallas as pl
from jax.experimental.pallas import tpu as pltpu

F32 = jnp.float32
BF16 = jnp.bfloat16

D_MODEL = 1024
HEAD_DIM = 64
ATTN_Q_HEADS = 8
ATTN_KV_HEADS = 4
ATTN_GROUP = ATTN_Q_HEADS // ATTN_KV_HEADS
DILATED_PATTERNS = ((128, 1), (512, 4), (2048, 16))
ATTN_BLOCK = 128
ATTN_DIM = ATTN_Q_HEADS * HEAD_DIM
KV_DIM = ATTN_KV_HEADS * HEAD_DIM
SSM_HEADS = 16
SSM_HEAD_DIM = 64
SSM_INNER = SSM_HEADS * SSM_HEAD_DIM
SSM_GROUPS = 2
SSM_STATE = 128
SSM_CONV = 4
SSM_CHUNK = 128
BC_DIM = SSM_GROUPS * SSM_STATE
XBC_DIM = SSM_INNER + 2 * BC_DIM
D_FF = 2816
FFN_CONV = 3
PLE_DIM = 256
EPS = 1e-6
LOG2E = 1.4426950408889634

LANES = 128
SUBLANES = 8
QK_DIM = ATTN_DIM + KV_DIM
QKV_DIM = QK_DIM + KV_DIM
QKV_SLABS = QKV_DIM // LANES
ATTN_SLABS = ATTN_DIM // LANES
KV_PAIRS = ATTN_KV_HEADS // 2
VMEM_LIMIT = 56 * 1024 * 1024

MAX_DIL = 16
SUPER = MAX_DIL * ATTN_BLOCK
IN_TM = 512
XBC_CHUNK = 512
TAIL_TM = 512
ROW_CLASSES = 8
FF_CHUNK = 256

assert DILATED_PATTERNS == ((128, 1), (512, 4), (2048, 16))
assert all(w // d == ATTN_BLOCK for w, d in DILATED_PATTERNS)


def _dot(a, b):
    return jnp.dot(a, b, preferred_element_type=F32)


def _dot_nt(a, b):
    return lax.dot_general(a, b, (((1,), (1,)), ((), ())), preferred_element_type=F32)


def _dot_tn(a, b):
    return lax.dot_general(a, b, (((0,), (0,)), ((), ())), preferred_element_type=F32)


def _rms(x):
    return x * lax.rsqrt(jnp.mean(x * x, axis=-1, keepdims=True) + EPS)


def _silu(x):
    return x * jax.nn.sigmoid(x)


def _const_spec(shape):
    nd = len(shape)
    return pl.BlockSpec(shape, lambda *_: (0,) * nd, pipeline_mode=pl.Buffered(1))


def _in_proj_kernel(tiles_per_batch,
                    x_ref, g_ref, wqk_ref, wv_ref, wz_ref, wxbc_ref, wdt_ref, qkg_ref, bd_ref,
                    cw_ref, cb_ref, dtb_ref,
                    qkv_ref, z_ref, xbc_ref, dt_ref, nat, mod4, *cbufs):
    i = pl.program_id(0)
    tm = x_ref.shape[0]

    @pl.when(i % tiles_per_batch == 0)
    def _():
        for cbuf in cbufs:
            cbuf[0:SUBLANES, :] = jnp.zeros((SUBLANES, XBC_CHUNK), F32)

    hb = (_rms(x_ref[...]) * g_ref[...]).astype(BF16)

    def ssm_conv(c):
        cbuf = cbufs[c]
        cs = slice(c * XBC_CHUNK, (c + 1) * XBC_CHUNK)
        cbuf[SUBLANES:SUBLANES + tm, :] = _dot(hb, wxbc_ref[:, cs])
        acc = cb_ref[:, cs] + cw_ref[SSM_CONV - 1:SSM_CONV, cs] * cbuf[SUBLANES:SUBLANES + tm, :]
        for kk in range(SSM_CONV - 1):
            off = SUBLANES - (SSM_CONV - 1) + kk
            acc = acc + cw_ref[kk:kk + 1, cs] * cbuf[off:off + tm, :]
        xbc_ref[:, cs] = _silu(acc)
        cbuf[0:SUBLANES, :] = cbuf[tm:tm + SUBLANES, :]

    def qk_norm():
        qk = _dot(hb, wqk_ref[...])
        sq = qk * qk
        sq_hi = sq.astype(BF16)
        sq_lo = (sq - sq_hi.astype(F32)).astype(BF16)
        bd = bd_ref[...]
        for c in range(QK_DIM // 256):
            sl = slice(c * 256, (c + 1) * 256)
            ss = _dot(sq_hi[:, sl], bd) + _dot(sq_lo[:, sl], bd)
            qkn = qk[:, sl] * lax.rsqrt(ss * (1.0 / HEAD_DIM) + EPS) * qkg_ref[:, sl]
            nat[2 * c] = qkn[:, :LANES]
            nat[2 * c + 1] = qkn[:, LANES:]

    def v_proj():
        v = _dot(hb, wv_ref[...])
        nat[QKV_SLABS - 2] = v[:, :LANES]
        nat[QKV_SLABS - 1] = v[:, LANES:]

    def relayout():
        quarter = tm // 4
        sixteenth = tm // MAX_DIL
        for s in range(QKV_SLABS):
            for a in range(4):
                mod4[s, a * quarter:(a + 1) * quarter, :] = nat[s, pl.ds(a, quarter, stride=4), :]
        k = i % (SUPER // tm)
        for s in range(QKV_SLABS):
            for a in range(4):
                for b in range(4):
                    dst = pl.multiple_of((4 * b + a) * ATTN_BLOCK + sixteenth * k, sixteenth)
                    qkv_ref[s, pl.ds(dst, sixteenth), :] = mod4[s, pl.ds(a * quarter + b, sixteenth, stride=4), :]

    def z_proj():
        z_ref[...] = _dot(hb, wz_ref[...]).astype(BF16)

    def dt_proj():
        dtr = _dot(hb, wdt_ref[...]) + dtb_ref[...]
        dt_ref[...] = jnp.maximum(dtr, 0.0) + jnp.log1p(jnp.exp(-jnp.abs(dtr)))

    for c in range(len(cbufs)):
        ssm_conv(c)
    qk_norm()
    v_proj()
    relayout()
    z_proj()
    dt_proj()


def _in_proj(x2d, g, wqk, wv, wz, wxbc, wdt, qkg, bd, cw, cb, dtb, seq):
    t = x2d.shape[0]
    tm = IN_TM
    assert SUPER % tm == 0 and tm % (4 * MAX_DIL) == 0 and seq % SUPER == 0
    grid = (t // tm,)
    row = lambda w: pl.BlockSpec((tm, w), lambda i: (i, 0))
    per_super = SUPER // tm
    return pl.pallas_call(
        functools.partial(_in_proj_kernel, seq // tm),
        grid=grid,
        in_specs=[row(D_MODEL), _const_spec(g.shape), _const_spec(wqk.shape), _const_spec(wv.shape),
                  _const_spec(wz.shape), _const_spec(wxbc.shape), _const_spec(wdt.shape),
                  _const_spec(qkg.shape), _const_spec(bd.shape), _const_spec(cw.shape),
                  _const_spec(cb.shape), _const_spec(dtb.shape)],
        out_specs=[pl.BlockSpec((QKV_SLABS, SUPER, LANES), lambda i: (0, i // per_super, 0)),
                   row(SSM_INNER), row(XBC_DIM), row(LANES)],
        out_shape=[jax.ShapeDtypeStruct((QKV_SLABS, t, LANES), F32), jax.ShapeDtypeStruct((t, SSM_INNER), BF16),
                   jax.ShapeDtypeStruct((t, XBC_DIM), F32), jax.ShapeDtypeStruct((t, LANES), F32)],
        scratch_shapes=[pltpu.VMEM((QKV_SLABS, tm, LANES), F32),
                        pltpu.VMEM((QKV_SLABS, tm, LANES), F32)]
                       + [pltpu.VMEM((tm + SUBLANES, XBC_CHUNK), F32)] * (XBC_DIM // XBC_CHUNK),
        compiler_params=pltpu.CompilerParams(dimension_semantics=("arbitrary",),
                                             vmem_limit_bytes=VMEM_LIMIT),
        name="in_proj",
    )(x2d, g, wqk, wv, wz, wxbc, wdt, qkg, bd, cw, cb, dtb)


def _block_perm(dil):
    reps = MAX_DIL // dil
    per = ATTN_BLOCK // reps
    pos = np.arange(ATTN_BLOCK)
    return (pos % per) * reps + pos // per


def _attn_bias():
    out = []
    for _, dil in DILATED_PATTERNS[::-1]:
        perm = _block_perm(dil)
        qi = perm[:, None]
        kj = np.concatenate([perm, ATTN_BLOCK + perm])[None, :]
        delta = ATTN_BLOCK + qi - kj
        band = (delta >= 0) & (delta <= ATTN_BLOCK)
        for first in (False, True):
            ok = band & (kj >= ATTN_BLOCK) if first else band
            out.append(np.tile(np.where(ok, 0.0, -np.inf).astype(np.float32), (4, 1)))
    return np.stack(out)


def _attn_kernel(cur_ref, prev_ref, bias_ref, out_ref, acc_o, acc_m, acc_l):
    first = (pl.program_id(1) == 0).astype(jnp.int32)
    lane = lax.broadcasted_iota(jnp.int32, (ATTN_BLOCK, LANES), 1)
    lo_half = lane < HEAD_DIM
    ones = jnp.ones((2 * ATTN_BLOCK, LANES), BF16)
    k_slab, v_slab = ATTN_SLABS, ATTN_SLABS + KV_PAIRS

    def rows(ref, slab, pieces):
        parts = [ref[slab, pl.ds(st, sz), :] for st, sz in pieces]
        return parts[0] if len(parts) == 1 else jnp.concatenate(parts, axis=0)

    def unit(q_pieces, prev_pieces, prev_in_cur, bias_idx, mode, out_start=None):
        bias = bias_ref[bias_idx]
        for p in range(KV_PAIRS):
            if prev_in_cur:
                kp, vp = rows(cur_ref, k_slab + p, prev_pieces), rows(cur_ref, v_slab + p, prev_pieces)
            else:
                kp, vp = rows(prev_ref, p, prev_pieces), rows(prev_ref, KV_PAIRS + p, prev_pieces)
            k2 = jnp.concatenate([kp, rows(cur_ref, k_slab + p, q_pieces)], axis=0).astype(BF16)
            v2 = jnp.concatenate([vp, rows(cur_ref, v_slab + p, q_pieces)], axis=0).astype(BF16)
            v2 = jnp.concatenate([v2, ones], axis=1)
            qs = []
            for g in range(ATTN_GROUP):
                q = rows(cur_ref, g * KV_PAIRS + p, q_pieces)
                qs += [jnp.where(lo_half, q, 0.0), jnp.where(lo_half, 0.0, q)]
            q4 = jnp.concatenate(qs, axis=0).astype(BF16)
            s = _dot_nt(q4, k2) + bias
            m = jnp.max(s, axis=-1, keepdims=True)
            pr = jnp.exp2(s - m).astype(BF16)
            ol = _dot(pr, v2)
            for g in range(ATTN_GROUP):
                r0, r1 = (2 * g) * ATTN_BLOCK, (2 * g + 1) * ATTN_BLOCK
                o = jnp.where(lo_half, ol[r0:r0 + ATTN_BLOCK, :LANES], ol[r1:r1 + ATTN_BLOCK, :LANES])
                l = jnp.where(lo_half, ol[r0:r0 + ATTN_BLOCK, LANES:], ol[r1:r1 + ATTN_BLOCK, LANES:])
                mm = jnp.where(lo_half, m[r0:r0 + ATTN_BLOCK], m[r1:r1 + ATTN_BLOCK])
                slab = g * KV_PAIRS + p
                if mode != "init":
                    m_old = rows(acc_m, slab, q_pieces)
                    m_new = jnp.maximum(m_old, mm)
                    w_old, w_new = jnp.exp2(m_old - m_new), jnp.exp2(mm - m_new)
                    o = rows(acc_o, slab, q_pieces) * w_old + o * w_new
                    l = rows(acc_l, slab, q_pieces) * w_old + l * w_new
                    mm = m_new
                if mode == "final":
                    res = o / l
                    for r in range(MAX_DIL):
                        out_ref[slab, pl.ds(out_start + r, SUBLANES, stride=MAX_DIL), :] = (
                            res[r * SUBLANES:(r + 1) * SUBLANES])
                else:
                    off = 0
                    for st, sz in q_pieces:
                        acc_o[slab, pl.ds(st, sz), :] = o[off:off + sz]
                        acc_m[slab, pl.ds(st, sz), :] = mm[off:off + sz]
                        acc_l[slab, pl.ds(st, sz), :] = l[off:off + sz]
                        off += sz

    def aligned(x, m):
        return x if isinstance(x, int) else pl.multiple_of(x, m)

    def d16_body(r, carry):
        st = pl.multiple_of(r * ATTN_BLOCK, ATTN_BLOCK)
        unit([(st, ATTN_BLOCK)], [(st, ATTN_BLOCK)], False, first, "init")
        return carry
    lax.fori_loop(0, MAX_DIL, d16_body, 0, unroll=4)

    def d4_pieces(c, row0):
        return [(aligned((4 * a + c) * ATTN_BLOCK + row0, 32), 32) for a in range(4)]

    def d4_body(c, carry):
        unit(d4_pieces(c, 0), d4_pieces(c, ATTN_BLOCK - 32), False, 2 + first, "merge")

        for n in range(1, 4):
            unit(d4_pieces(c, 32 * n), d4_pieces(c, 32 * (n - 1)), True, 2, "merge")
        return carry
    lax.fori_loop(0, 4, d4_body, 0)

    def d1_pieces(row0):
        return [(aligned(r * ATTN_BLOCK + row0, SUBLANES), SUBLANES) for r in range(MAX_DIL)]

    unit(d1_pieces(0), d1_pieces(ATTN_BLOCK - SUBLANES), False, 4 + first, "final", 0)

    def d1_body(n, carry):
        unit(d1_pieces(SUBLANES * n), d1_pieces(SUBLANES * (n - 1)), True, 4, "final", n * ATTN_BLOCK)
        return carry
    lax.fori_loop(1, SUPER // ATTN_BLOCK, d1_body, 0, unroll=3)


def _attention(qkv, bias, bsz, seq):
    t = bsz * seq
    nsb = seq // SUPER
    acc = pltpu.VMEM((ATTN_SLABS, SUPER, LANES), F32)
    return pl.pallas_call(
        _attn_kernel,
        grid=(bsz, nsb),
        in_specs=[pl.BlockSpec((QKV_SLABS, SUPER, LANES), lambda b, s: (0, b * nsb + s, 0)),
                  pl.BlockSpec((QKV_SLABS // 2, SUPER, LANES),
                               lambda b, s: (1, b * nsb + jnp.maximum(s - 1, 0), 0)),
                  _const_spec(bias.shape)],
        out_specs=pl.BlockSpec((ATTN_SLABS, SUPER, LANES), lambda b, s: (0, b * nsb + s, 0)),
        out_shape=jax.ShapeDtypeStruct((ATTN_SLABS, t, LANES), F32),
        scratch_shapes=[acc, acc, acc],
        compiler_params=pltpu.CompilerParams(dimension_semantics=("arbitrary", "arbitrary"),
                                             vmem_limit_bytes=VMEM_LIMIT),
        name="attention",
    )(qkv, qkv, bias)


def _ssd_kernel(xbc_ref, dt_ref, z_ref, alog_ref, expand_ref, dskip_ref, ng_ref, out_ref, state_ref, y_ref):
    c = pl.program_id(1)
    ln = SSM_CHUNK

    @pl.when(c == 0)
    def _():
        state_ref[...] = jnp.zeros_like(state_ref)

    lane = lax.broadcasted_iota(jnp.int32, (ln, LANES), 1)
    lo_half = lane < SSM_HEAD_DIM
    part = lane // SSM_HEADS
    row = lax.broadcasted_iota(jnp.int32, (ln, ln), 0)
    col = lax.broadcasted_iota(jnp.int32, (ln, ln), 1)
    causal = row >= col
    tril = causal.astype(BF16)

    def split3(v):
        hi = v.astype(BF16).astype(F32)
        mid = (v - hi).astype(BF16).astype(F32)
        lo = (v - hi) - mid
        return hi, mid, lo

    def pack3(v):
        hi, mid, lo = split3(v)
        return jnp.where(part == 0, hi, jnp.where(part == 1, mid, jnp.where(part == 2, lo, 0.0))).astype(BF16)

    xs = xbc_ref[:, 0:SSM_INNER]
    dt = dt_ref[...]
    a = dt * (-jnp.exp(alog_ref[...]))
    a3 = jnp.concatenate(split3(a), axis=0).astype(BF16)
    acum = _dot(jnp.concatenate([tril, tril, tril], axis=1), a3)
    acum_t = acum.T
    total = acum[ln - 1:ln, :]
    packed = jnp.concatenate([pack3(dt), pack3(jnp.exp(acum)), pack3(jnp.exp(total - acum))], axis=0)
    expanded = _dot(packed, expand_ref[...])
    dt_e, ea_e, de_e = expanded[0:ln], expanded[ln:2 * ln], expanded[2 * ln:3 * ln]
    xdt = xs * dt_e
    xde = (xdt * de_e).astype(BF16)

    gw = SSM_INNER // SSM_GROUPS
    for g in range(SSM_GROUPS):
        bg = xbc_ref[:, SSM_INNER + g * SSM_STATE:SSM_INNER + (g + 1) * SSM_STATE].astype(BF16)
        cg = xbc_ref[:, SSM_INNER + BC_DIM + g * SSM_STATE:SSM_INNER + BC_DIM + (g + 1) * SSM_STATE].astype(BF16)
        cb = _dot_nt(cg, bg)
        for j in range(SSM_HEADS // SSM_GROUPS // 2):
            h0 = g * (SSM_HEADS // SSM_GROUPS) + 2 * j
            ms = []
            for h in (h0, h0 + 1):
                seg = acum[:, h:h + 1] - acum_t[h:h + 1, :]
                ms.append(cb * jnp.exp(jnp.where(causal, seg, -jnp.inf)))
            mm = jnp.concatenate(ms, axis=1).astype(BF16)
            xp = xdt[:, h0 * SSM_HEAD_DIM:(h0 + 2) * SSM_HEAD_DIM]
            rhs = jnp.concatenate([jnp.where(lo_half, xp, 0.0), jnp.where(lo_half, 0.0, xp)],
                                  axis=0).astype(BF16)
            y_ref[:, h0 * SSM_HEAD_DIM:(h0 + 2) * SSM_HEAD_DIM] = _dot(mm, rhs)
        gs = slice(g * gw, (g + 1) * gw)
        st = state_ref[g]
        y_ref[:, gs] = y_ref[:, gs] + _dot(cg, st.astype(BF16)) * ea_e[:, gs]
        state_ref[g] = st * ea_e[ln - 1:ln, gs] + _dot_tn(bg, xde[:, gs])

    y = y_ref[...] + dskip_ref[...] * xs
    y = y * _silu(z_ref[...].astype(F32))
    out_ref[...] = (_rms(y) * ng_ref[...]).astype(BF16)


def _ssd(xbc, dt, z, alog, expand, dskip, ng, bsz, seq):
    t = bsz * seq
    nc = seq // SSM_CHUNK
    row = lambda w: pl.BlockSpec((SSM_CHUNK, w), lambda b, c: (b * nc + c, 0))
    return pl.pallas_call(
        _ssd_kernel,
        grid=(bsz, nc),
        in_specs=[row(XBC_DIM), row(LANES), row(SSM_INNER), _const_spec(alog.shape),
                  _const_spec(expand.shape), _const_spec(dskip.shape), _const_spec(ng.shape)],
        out_specs=row(SSM_INNER),
        out_shape=jax.ShapeDtypeStruct((t, SSM_INNER), BF16),
        scratch_shapes=[pltpu.VMEM((SSM_GROUPS, SSM_STATE, SSM_INNER // SSM_GROUPS), F32),
                        pltpu.VMEM((SSM_CHUNK, SSM_INNER), F32)],
        compiler_params=pltpu.CompilerParams(dimension_semantics=("arbitrary", "arbitrary"),
                                             vmem_limit_bytes=VMEM_LIMIT),
        name="ssd",
    )(xbc, dt, z, alog, expand, dskip, ng)


def _tail_kernel(tiles_per_batch,
                 x_ref, attn_ref, ssm_ref, p_ref, woa_ref, wos_ref, fg_ref, wup_ref, fcw_ref, fcb_ref, wdn_ref,
                 pg_ref, wpg_ref, wpp_ref, out_ref, slabs, act_ref, carry, sbuf):
    i = pl.program_id(0)
    tm = x_ref.shape[0]
    n8 = tm // ROW_CLASSES
    nslab = D_MODEL // LANES

    @pl.when(i % tiles_per_batch == 0)
    def _():
        carry[...] = jnp.zeros_like(carry)

    attn = jnp.concatenate([attn_ref[s] for s in range(ATTN_SLABS)], axis=1).astype(BF16)
    x1 = x_ref[...] + _dot(ssm_ref[...], wos_ref[...]) + _dot(attn, woa_ref[...])

    h2 = _rms(x1) * fg_ref[...]
    for s in range(nslab):
        slabs[s] = h2[:, s * LANES:(s + 1) * LANES]
    h2p = jnp.concatenate(
        [jnp.concatenate([slabs[s, pl.ds(q, n8, stride=ROW_CLASSES), :] for q in range(ROW_CLASSES)], axis=0)
         for s in range(nslab)], axis=1).astype(BF16)

    def conv(u, cs, slot):
        blk = [u[q * n8:(q + 1) * n8] for q in range(ROW_CLASSES)]
        shifted = []
        for k in range(FFN_CONV - 1):
            sb = sbuf.at[slot * (FFN_CONV - 1) + k]
            sb[0:SUBLANES, :] = carry[k, :, cs]
            sb[SUBLANES:SUBLANES + n8, :] = blk[ROW_CLASSES - (FFN_CONV - 1) + k]
            shifted.append(sb[SUBLANES - 1:SUBLANES - 1 + n8, :])
            carry[k, :, cs] = sb[n8:n8 + SUBLANES, :]
        prev = shifted + blk
        w = [fcw_ref[k:k + 1, cs] for k in range(FFN_CONV)]
        b = fcb_ref[:, cs]
        return jnp.concatenate([b + w[2] * prev[q + 2] + w[1] * prev[q + 1] + w[0] * prev[q]
                                for q in range(ROW_CLASSES)], axis=0)

    for c in range(D_FF // FF_CHUNK):
        gs = slice(c * FF_CHUNK, (c + 1) * FF_CHUNK)
        vs = slice(D_FF + c * FF_CHUNK, D_FF + (c + 1) * FF_CHUNK)
        gate = conv(_dot(h2p, wup_ref[:, gs]), gs, 0)
        val = conv(_dot(h2p, wup_ref[:, vs]), vs, 1)
        act_ref[:, gs] = (_silu(gate) * val).astype(BF16)
    ffn_p = _dot(act_ref[...], wdn_ref[...])

    for s in range(nslab):
        for q in range(ROW_CLASSES):
            slabs[s, pl.ds(q, n8, stride=ROW_CLASSES), :] = ffn_p[q * n8:(q + 1) * n8, s * LANES:(s + 1) * LANES]
    x2 = x1 + jnp.concatenate([slabs[s] for s in range(nslab)], axis=1)

    h3 = (_rms(x2) * pg_ref[...]).astype(BF16)
    gate = jax.nn.sigmoid(_dot(h3, wpg_ref[...]))
    out_ref[...] = x2 + gate * _dot(p_ref[...].astype(BF16), wpp_ref[...])


def _tail(x2d, attn, ssm, p2d, woa, wos, fg, wup, fcw, fcb, wdn, pg, wpg, wpp, seq):
    t = x2d.shape[0]
    tm = TAIL_TM
    assert seq % tm == 0 and tm % (ROW_CLASSES * SUBLANES) == 0 and D_FF % FF_CHUNK == 0
    row = lambda w: pl.BlockSpec((tm, w), lambda i: (i, 0))
    consts = [woa, wos, fg, wup, fcw, fcb, wdn, pg, wpg, wpp]
    return pl.pallas_call(
        functools.partial(_tail_kernel, seq // tm),
        grid=(t // tm,),
        in_specs=[row(D_MODEL), pl.BlockSpec((ATTN_SLABS, tm, LANES), lambda i: (0, i, 0)),
                  row(SSM_INNER), row(PLE_DIM)] + [_const_spec(w.shape) for w in consts],
        out_specs=row(D_MODEL),
        out_shape=jax.ShapeDtypeStruct((t, D_MODEL), F32),
        scratch_shapes=[pltpu.VMEM((D_MODEL // LANES, tm, LANES), F32),
                        pltpu.VMEM((tm, D_FF), BF16),
                        pltpu.VMEM((FFN_CONV - 1, SUBLANES, 2 * D_FF), F32),
                        pltpu.VMEM((2 * (FFN_CONV - 1), SUBLANES + tm // ROW_CLASSES, FF_CHUNK), F32)],
        compiler_params=pltpu.CompilerParams(dimension_semantics=("arbitrary",),
                                             vmem_limit_bytes=VMEM_LIMIT),
        name="tail",
    )(x2d, attn, ssm, p2d, *consts)


def _q_perm():
    idx = np.arange(ATTN_DIM).reshape(ATTN_KV_HEADS, ATTN_GROUP, HEAD_DIM)
    return np.transpose(idx, (1, 0, 2)).reshape(-1)


def _layer(x, p, attn_norm_g, w_in, q_norm_g, k_norm_g, ssm_conv_w, ssm_conv_b, dt_bias, a_log,
           d_skip, ssm_norm_g, w_out, ffn_norm_g, w_up, ffn_conv_w, ffn_conv_b, w_down,
           ple_norm_g, w_ple_gate, w_ple_proj):
    bsz, seq, _ = x.shape
    t = bsz * seq
    x2d = x.reshape(t, D_MODEL)
    p2d = p.reshape(t, PLE_DIM)
    qperm = _q_perm()

    o_k, o_v, o_z = ATTN_DIM, ATTN_DIM + KV_DIM, ATTN_DIM + 2 * KV_DIM
    o_xbc, o_dt = o_z + SSM_INNER, o_z + SSM_INNER + XBC_DIM
    wqk = jnp.concatenate([w_in[:, :ATTN_DIM][:, qperm], w_in[:, o_k:o_v]], axis=1).astype(BF16)
    wv = w_in[:, o_v:o_z].astype(BF16)
    wz = w_in[:, o_z:o_xbc].astype(BF16)
    wxbc = w_in[:, o_xbc:o_dt].astype(BF16)
    head_reps = LANES // SSM_HEADS
    wdt = jnp.tile(w_in[:, o_dt:], (1, head_reps)).astype(BF16)
    qkg = jnp.concatenate([jnp.tile(q_norm_g * (HEAD_DIM ** -0.5 * LOG2E), ATTN_Q_HEADS),
                           jnp.tile(k_norm_g, ATTN_KV_HEADS)]).reshape(1, QK_DIM)
    hid = np.arange(256) // HEAD_DIM
    bd = jnp.asarray(hid[:, None] == hid[None, :], BF16)
    dtb = jnp.tile(dt_bias, head_reps).reshape(1, LANES)
    alog = jnp.tile(a_log, head_reps).reshape(1, LANES)

    qkv, z, xbc, dt = _in_proj(x2d, attn_norm_g.reshape(1, D_MODEL), wqk, wv, wz, wxbc, wdt, qkg, bd,
                               ssm_conv_w, ssm_conv_b.reshape(1, XBC_DIM), dtb, seq)

    attn = _attention(qkv, jnp.asarray(_attn_bias()), bsz, seq)

    src_lane = np.arange(LANES)[:, None]
    expand = jnp.asarray((src_lane % SSM_HEADS == (np.arange(SSM_INNER) // SSM_HEAD_DIM)[None, :])
                         & (src_lane < 3 * SSM_HEADS), BF16)
    dskip = jnp.repeat(d_skip, SSM_HEAD_DIM).reshape(1, SSM_INNER)
    ssm = _ssd(xbc, dt, z, alog, expand, dskip, ssm_norm_g.reshape(1, SSM_INNER), bsz, seq)

    woa = w_out[:ATTN_DIM][qperm].astype(BF16)
    wos = w_out[ATTN_DIM:].astype(BF16)
    out = _tail(x2d, attn, ssm, p2d, woa, wos, ffn_norm_g.reshape(1, D_MODEL), w_up.astype(BF16),
                ffn_conv_w, ffn_conv_b.reshape(1, 2 * D_FF), w_down.astype(BF16),
                ple_norm_g.reshape(1, D_MODEL), w_ple_gate.astype(BF16), w_ple_proj.astype(BF16), seq)
    return out.reshape(bsz, seq, D_MODEL)


def kernel(x, p, attn_norm_g, w_in, q_norm_g, k_norm_g, ssm_conv_w, ssm_conv_b, dt_bias, a_log, d_skip,
           ssm_norm_g, w_out, ffn_norm_g, w_up, ffn_conv_w, ffn_conv_b, w_down, ple_norm_g, w_ple_gate,
           w_ple_proj):
    depth = w_in.shape[0]
    for i in range(depth):
        x = _layer(x, p[i], attn_norm_g[i], w_in[i], q_norm_g[i], k_norm_g[i], ssm_conv_w[i], ssm_conv_b[i],
                   dt_bias[i], a_log[i], d_skip[i], ssm_norm_g[i], w_out[i], ffn_norm_g[i], w_up[i],
                   ffn_conv_w[i], ffn_conv_b[i], w_down[i], ple_norm_g[i], w_ple_gate[i], w_ple_proj[i])
    return x
```

```python
import functools

import jax
import jax.numpy as jnp
import numpy as np
from jax import lax
from jax.experimental import pallas as pl
from jax.experimental.pallas import tpu as pltpu

F32 = jnp.float32
BF16 = jnp.bfloat16

D_MODEL = 1024
HEAD_DIM = 64
ATTN_Q_HEADS = 8
ATTN_KV_HEADS = 4
ATTN_GROUP = ATTN_Q_HEADS // ATTN_KV_HEADS
DILATED_PATTERNS = ((128, 1), (512, 4), (2048, 16))
ATTN_BLOCK = 128
ATTN_DIM = ATTN_Q_HEADS * HEAD_DIM
KV_DIM = ATTN_KV_HEADS * HEAD_DIM
SSM_HEADS = 16
SSM_HEAD_DIM = 64
SSM_INNER = SSM_HEADS * SSM_HEAD_DIM
SSM_GROUPS = 2
SSM_STATE = 128
SSM_CONV = 4
SSM_CHUNK = 128
BC_DIM = SSM_GROUPS * SSM_STATE
XBC_DIM = SSM_INNER + 2 * BC_DIM
D_FF = 2816
FFN_CONV = 3
PLE_DIM = 256
EPS = 1e-6
LOG2E = 1.4426950408889634

LANES = 128
SUBLANES = 8
QK_DIM = ATTN_DIM + KV_DIM
QKV_DIM = QK_DIM + KV_DIM
QKV_SLABS = QKV_DIM // LANES
ATTN_SLABS = ATTN_DIM // LANES
KV_PAIRS = ATTN_KV_HEADS // 2
VMEM_LIMIT = 56 * 1024 * 1024

MAX_DIL = 16
SUPER = MAX_DIL * ATTN_BLOCK
IN_TM = 512
XBC_CHUNK = 512
TAIL_TM = 512
ROW_CLASSES = 8
FF_CHUNK = 256

assert DILATED_PATTERNS == ((128, 1), (512, 4), (2048, 16))
assert all(w // d == ATTN_BLOCK for w, d in DILATED_PATTERNS)


def _dot(a, b):
    return jnp.dot(a, b, preferred_element_type=F32)


def _dot_nt(a, b):
    return lax.dot_general(a, b, (((1,), (1,)), ((), ())), preferred_element_type=F32)


def _dot_tn(a, b):
    return lax.dot_general(a, b, (((0,), (0,)), ((), ())), preferred_element_type=F32)


def _rms(x):
    return x * lax.rsqrt(jnp.mean(x * x, axis=-1, keepdims=True) + EPS)


def _silu(x):
    return x * jax.nn.sigmoid(x)


def _const_spec(shape):
    nd = len(shape)
    return pl.BlockSpec(shape, lambda *_: (0,) * nd, pipeline_mode=pl.Buffered(1))


def _in_proj_kernel(tiles_per_batch,
                    x_ref, g_ref, wqk_ref, wv_ref, wz_ref, wxbc_ref, wdt_ref, qkg_ref, bd_ref,
                    cw_ref, cb_ref, dtb_ref,
                    qkv_ref, z_ref, xbc_ref, dt_ref, nat, mod4, *cbufs):
    i = pl.program_id(0)
    tm = x_ref.shape[0]

    @pl.when(i % tiles_per_batch == 0)
    def _():
        for cbuf in cbufs:
            cbuf[0:SUBLANES, :] = jnp.zeros((SUBLANES, XBC_CHUNK), F32)

    hb = (_rms(x_ref[...]) * g_ref[...]).astype(BF16)

    def ssm_conv(c):
        cbuf = cbufs[c]
        cs = slice(c * XBC_CHUNK, (c + 1) * XBC_CHUNK)
        cbuf[SUBLANES:SUBLANES + tm, :] = _dot(hb, wxbc_ref[:, cs])
        acc = cb_ref[:, cs] + cw_ref[SSM_CONV - 1:SSM_CONV, cs] * cbuf[SUBLANES:SUBLANES + tm, :]
        for kk in range(SSM_CONV - 1):
            off = SUBLANES - (SSM_CONV - 1) + kk
            acc = acc + cw_ref[kk:kk + 1, cs] * cbuf[off:off + tm, :]
        xbc_ref[:, cs] = _silu(acc)
        cbuf[0:SUBLANES, :] = cbuf[tm:tm + SUBLANES, :]

    def qk_norm():
        qk = _dot(hb, wqk_ref[...])
        sq = qk * qk
        sq_hi = sq.astype(BF16)
        sq_lo = (sq - sq_hi.astype(F32)).astype(BF16)
        bd = bd_ref[...]
        for c in range(QK_DIM // 256):
            sl = slice(c * 256, (c + 1) * 256)
            ss = _dot(sq_hi[:, sl], bd) + _dot(sq_lo[:, sl], bd)
            qkn = qk[:, sl] * lax.rsqrt(ss * (1.0 / HEAD_DIM) + EPS) * qkg_ref[:, sl]
            nat[2 * c] = qkn[:, :LANES]
            nat[2 * c + 1] = qkn[:, LANES:]

    def v_proj():
        v = _dot(hb, wv_ref[...])
        nat[QKV_SLABS - 2] = v[:, :LANES]
        nat[QKV_SLABS - 1] = v[:, LANES:]

    def relayout():
        quarter = tm // 4
        sixteenth = tm // MAX_DIL
        for s in range(QKV_SLABS):
            for a in range(4):
                mod4[s, a * quarter:(a + 1) * quarter, :] = nat[s, pl.ds(a, quarter, stride=4), :]
        k = i % (SUPER // tm)
        for s in range(QKV_SLABS):
            for a in range(4):
                for b in range(4):
                    dst = pl.multiple_of((4 * b + a) * ATTN_BLOCK + sixteenth * k, sixteenth)
                    qkv_ref[s, pl.ds(dst, sixteenth), :] = mod4[s, pl.ds(a * quarter + b, sixteenth, stride=4), :]

    def z_proj():
        z_ref[...] = _dot(hb, wz_ref[...]).astype(BF16)

    def dt_proj():
        dtr = _dot(hb, wdt_ref[...]) + dtb_ref[...]
        dt_ref[...] = jnp.maximum(dtr, 0.0) + jnp.log1p(jnp.exp(-jnp.abs(dtr)))

    for c in range(len(cbufs)):
        ssm_conv(c)
    qk_norm()
    v_proj()
    relayout()
    z_proj()
    dt_proj()


def _in_proj(x2d, g, wqk, wv, wz, wxbc, wdt, qkg, bd, cw, cb, dtb, seq):
    t = x2d.shape[0]
    tm = IN_TM
    assert SUPER % tm == 0 and tm % (4 * MAX_DIL) == 0 and seq % SUPER == 0
    grid = (t // tm,)
    row = lambda w: pl.BlockSpec((tm, w), lambda i: (i, 0))
    per_super = SUPER // tm
    return pl.pallas_call(
        functools.partial(_in_proj_kernel, seq // tm),
        grid=grid,
        in_specs=[row(D_MODEL), _const_spec(g.shape), _const_spec(wqk.shape), _const_spec(wv.shape),
                  _const_spec(wz.shape), _const_spec(wxbc.shape), _const_spec(wdt.shape),
                  _const_spec(qkg.shape), _const_spec(bd.shape), _const_spec(cw.shape),
                  _const_spec(cb.shape), _const_spec(dtb.shape)],
        out_specs=[pl.BlockSpec((QKV_SLABS, SUPER, LANES), lambda i: (0, i // per_super, 0)),
                   row(SSM_INNER), row(XBC_DIM), row(LANES)],
        out_shape=[jax.ShapeDtypeStruct((QKV_SLABS, t, LANES), F32), jax.ShapeDtypeStruct((t, SSM_INNER), BF16),
                   jax.ShapeDtypeStruct((t, XBC_DIM), F32), jax.ShapeDtypeStruct((t, LANES), F32)],
        scratch_shapes=[pltpu.VMEM((QKV_SLABS, tm, LANES), F32),
                        pltpu.VMEM((QKV_SLABS, tm, LANES), F32)]
                       + [pltpu.VMEM((tm + SUBLANES, XBC_CHUNK), F32)] * (XBC_DIM // XBC_CHUNK),
        compiler_params=pltpu.CompilerParams(dimension_semantics=("arbitrary",),
                                             vmem_limit_bytes=VMEM_LIMIT),
        name="in_proj",
    )(x2d, g, wqk, wv, wz, wxbc, wdt, qkg, bd, cw, cb, dtb)


def _block_perm(dil):
    reps = MAX_DIL // dil
    per = ATTN_BLOCK // reps
    pos = np.arange(ATTN_BLOCK)
    return (pos % per) * reps + pos // per


def _attn_bias():
    out = []
    for _, dil in DILATED_PATTERNS[::-1]:
        perm = _block_perm(dil)
        qi = perm[:, None]
        kj = np.concatenate([perm, ATTN_BLOCK + perm])[None, :]
        delta = ATTN_BLOCK + qi - kj
        band = (delta >= 0) & (delta <= ATTN_BLOCK)
        for first in (False, True):
            ok = band & (kj >= ATTN_BLOCK) if first else band
            out.append(np.tile(np.where(ok, 0.0, -np.inf).astype(np.float32), (4, 1)))
    return np.stack(out)


def _attn_kernel(cur_ref, prev_ref, bias_ref, out_ref, acc_o, acc_m, acc_l):
    first = (pl.program_id(1) == 0).astype(jnp.int32)
    lane = lax.broadcasted_iota(jnp.int32, (ATTN_BLOCK, LANES), 1)
    lo_half = lane < HEAD_DIM
    ones = jnp.ones((2 * ATTN_BLOCK, LANES), BF16)
    k_slab, v_slab = ATTN_SLABS, ATTN_SLABS + KV_PAIRS

    def rows(ref, slab, pieces):
        parts = [ref[slab, pl.ds(st, sz), :] for st, sz in pieces]
        return parts[0] if len(parts) == 1 else jnp.concatenate(parts, axis=0)

    def unit(q_pieces, prev_pieces, prev_in_cur, bias_idx, mode, out_start=None):
        bias = bias_ref[bias_idx]
        for p in range(KV_PAIRS):
            if prev_in_cur:
                kp, vp = rows(cur_ref, k_slab + p, prev_pieces), rows(cur_ref, v_slab + p, prev_pieces)
            else:
                kp, vp = rows(prev_ref, p, prev_pieces), rows(prev_ref, KV_PAIRS + p, prev_pieces)
            k2 = jnp.concatenate([kp, rows(cur_ref, k_slab + p, q_pieces)], axis=0).astype(BF16)
            v2 = jnp.concatenate([vp, rows(cur_ref, v_slab + p, q_pieces)], axis=0).astype(BF16)
            v2 = jnp.concatenate([v2, ones], axis=1)
            qs = []
            for g in range(ATTN_GROUP):
                q = rows(cur_ref, g * KV_PAIRS + p, q_pieces)
                qs += [jnp.where(lo_half, q, 0.0), jnp.where(lo_half, 0.0, q)]
            q4 = jnp.concatenate(qs, axis=0).astype(BF16)
            s = _dot_nt(q4, k2) + bias
            m = jnp.max(s, axis=-1, keepdims=True)
            pr = jnp.exp2(s - m).astype(BF16)
            ol = _dot(pr, v2)
            for g in range(ATTN_GROUP):
                r0, r1 = (2 * g) * ATTN_BLOCK, (2 * g + 1) * ATTN_BLOCK
                o = jnp.where(lo_half, ol[r0:r0 + ATTN_BLOCK, :LANES], ol[r1:r1 + ATTN_BLOCK, :LANES])
                l = jnp.where(lo_half, ol[r0:r0 + ATTN_BLOCK, LANES:], ol[r1:r1 + ATTN_BLOCK, LANES:])
                mm = jnp.where(lo_half, m[r0:r0 + ATTN_BLOCK], m[r1:r1 + ATTN_BLOCK])
                slab = g * KV_PAIRS + p
                if mode != "init":
                    m_old = rows(acc_m, slab, q_pieces)
                    m_new = jnp.maximum(m_old, mm)
                    w_old, w_new = jnp.exp2(m_old - m_new), jnp.exp2(mm - m_new)
                    o = rows(acc_o, slab, q_pieces) * w_old + o * w_new
                    l = rows(acc_l, slab, q_pieces) * w_old + l * w_new
                    mm = m_new
                if mode == "final":
                    res = o / l
                    for r in range(MAX_DIL):
                        out_ref[slab, pl.ds(out_start + r, SUBLANES, stride=MAX_DIL), :] = (
                            res[r * SUBLANES:(r + 1) * SUBLANES])
                else:
                    off = 0
                    for st, sz in q_pieces:
                        acc_o[slab, pl.ds(st, sz), :] = o[off:off + sz]
                        acc_m[slab, pl.ds(st, sz), :] = mm[off:off + sz]
                        acc_l[slab, pl.ds(st, sz), :] = l[off:off + sz]
                        off += sz

    def aligned(x, m):
        return x if isinstance(x, int) else pl.multiple_of(x, m)

    def d16_body(r, carry):
        st = pl.multiple_of(r * ATTN_BLOCK, ATTN_BLOCK)
        unit([(st, ATTN_BLOCK)], [(st, ATTN_BLOCK)], False, first, "init")
        return carry
    lax.fori_loop(0, MAX_DIL, d16_body, 0, unroll=4)

    def d4_pieces(c, row0):
        return [(aligned((4 * a + c) * ATTN_BLOCK + row0, 32), 32) for a in range(4)]

    def d4_body(c, carry):
        unit(d4_pieces(c, 0), d4_pieces(c, ATTN_BLOCK - 32), False, 2 + first, "merge")

        for n in range(1, 4):
            unit(d4_pieces(c, 32 * n), d4_pieces(c, 32 * (n - 1)), True, 2, "merge")
        return carry
    lax.fori_loop(0, 4, d4_body, 0)

    def d1_pieces(row0):
        return [(aligned(r * ATTN_BLOCK + row0, SUBLANES), SUBLANES) for r in range(MAX_DIL)]

    unit(d1_pieces(0), d1_pieces(ATTN_BLOCK - SUBLANES), False, 4 + first, "final", 0)

    def d1_body(n, carry):
        unit(d1_pieces(SUBLANES * n), d1_pieces(SUBLANES * (n - 1)), True, 4, "final", n * ATTN_BLOCK)
        return carry
    lax.fori_loop(1, SUPER // ATTN_BLOCK, d1_body, 0, unroll=3)


def _attention(qkv, bias, bsz, seq):
    t = bsz * seq
    nsb = seq // SUPER
    acc = pltpu.VMEM((ATTN_SLABS, SUPER, LANES), F32)
    return pl.pallas_call(
        _attn_kernel,
        grid=(bsz, nsb),
        in_specs=[pl.BlockSpec((QKV_SLABS, SUPER, LANES), lambda b, s: (0, b * nsb + s, 0)),
                  pl.BlockSpec((QKV_SLABS // 2, SUPER, LANES),
                               lambda b, s: (1, b * nsb + jnp.maximum(s - 1, 0), 0)),
                  _const_spec(bias.shape)],
        out_specs=pl.BlockSpec((ATTN_SLABS, SUPER, LANES), lambda b, s: (0, b * nsb + s, 0)),
        out_shape=jax.ShapeDtypeStruct((ATTN_SLABS, t, LANES), F32),
        scratch_shapes=[acc, acc, acc],
        compiler_params=pltpu.CompilerParams(dimension_semantics=("arbitrary", "arbitrary"),
                                             vmem_limit_bytes=VMEM_LIMIT),
        name="attention",
    )(qkv, qkv, bias)


def _ssd_kernel(xbc_ref, dt_ref, dtn_ref, z_ref, alog_ref, expand_ref, dskip_ref, ng_ref, out_ref,
                state_ref, y_ref, xde_ref, e_ref, *decay_refs):
    cur, nxt = decay_refs[:3], decay_refs[3:]
    c = pl.program_id(1)
    ln = SSM_CHUNK

    lane = lax.broadcasted_iota(jnp.int32, (ln, LANES), 1)
    lo_half = lane < SSM_HEAD_DIM
    part = lane // SSM_HEADS
    row = lax.broadcasted_iota(jnp.int32, (ln, ln), 0)
    col = lax.broadcasted_iota(jnp.int32, (ln, ln), 1)
    causal = row >= col
    tril = causal.astype(BF16)

    def split3(v):
        hi = v.astype(BF16).astype(F32)
        mid = (v - hi).astype(BF16).astype(F32)
        lo = (v - hi) - mid
        return hi, mid, lo

    def pack3(v):
        hi, mid, lo = split3(v)
        return jnp.where(part == 0, hi, jnp.where(part == 1, mid, jnp.where(part == 2, lo, 0.0))).astype(BF16)

    def decay_chain(dt, dst):
        a = dt * (-LOG2E * jnp.exp(alog_ref[...]))
        a3 = jnp.concatenate(split3(a), axis=0).astype(BF16)
        acum = _dot(jnp.concatenate([tril, tril, tril], axis=1), a3)
        total = acum[ln - 1:ln, :]
        packed = jnp.concatenate([pack3(dt), pack3(jnp.exp2(acum)), pack3(jnp.exp2(total - acum))], axis=0)
        packed_ref, acum_ref, acumt_ref = dst
        packed_ref[...] = packed
        acum_ref[...] = acum
        acumt_ref[...] = acum.T

    @pl.when(c == 0)
    def _():
        state_ref[...] = jnp.zeros_like(state_ref)
        decay_chain(dt_ref[...], cur)

    decay_chain(dtn_ref[...], nxt)

    bgs, cbs, y_offs = [], [], []
    for g in range(SSM_GROUPS):
        bg = xbc_ref[:, SSM_INNER + g * SSM_STATE:SSM_INNER + (g + 1) * SSM_STATE].astype(BF16)
        cg = xbc_ref[:, SSM_INNER + BC_DIM + g * SSM_STATE:SSM_INNER + BC_DIM + (g + 1) * SSM_STATE].astype(BF16)
        bgs.append(bg)
        cbs.append(_dot_nt(cg, bg))
        y_offs.append(_dot(cg, state_ref[g].astype(BF16)))

    e_ref[...] = _dot(cur[0][...], expand_ref[...])
    acum = cur[1][...]
    acum_t = cur[2][...]
    gw = SSM_INNER // SSM_GROUPS
    pairs = SSM_HEADS // SSM_GROUPS // 2
    ssq = jnp.zeros((ln, LANES), F32)
    for g in range(SSM_GROUPS):
        for j in range(pairs):
            h0 = g * 2 * pairs + 2 * j
            ps = slice(h0 * SSM_HEAD_DIM, (h0 + 2) * SSM_HEAD_DIM)
            xs = xbc_ref[:, ps]
            xdt = xs * e_ref[0:ln, ps]
            ms = []
            for h in (h0, h0 + 1):
                seg = acum[:, h:h + 1] - acum_t[h:h + 1, :]
                ms.append(cbs[g] * jnp.exp2(jnp.where(causal, seg, -jnp.inf)))
            mm = jnp.concatenate(ms, axis=1).astype(BF16)
            rhs = jnp.concatenate([jnp.where(lo_half, xdt, 0.0), jnp.where(lo_half, 0.0, xdt)],
                                  axis=0).astype(BF16)
            y = (_dot(mm, rhs) + y_offs[g][:, j * LANES:(j + 1) * LANES] * e_ref[ln:2 * ln, ps]
                 + dskip_ref[:, ps] * xs)
            y = y * _silu(z_ref[:, ps].astype(F32))
            y_ref[:, ps] = y
            ssq = ssq + y * y
            xde_ref[:, ps] = (xdt * e_ref[2 * ln:3 * ln, ps]).astype(BF16)
        gs = slice(g * gw, (g + 1) * gw)
        state_ref[g] = state_ref[g] * e_ref[2 * ln - 1:2 * ln, gs] + _dot_tn(bgs[g], xde_ref[:, gs])

    scale = lax.rsqrt(jnp.sum(ssq, axis=-1, keepdims=True) * (1.0 / SSM_INNER) + EPS)
    out_ref[...] = (y_ref[...] * scale * ng_ref[...]).astype(BF16)
    for dst, src in zip(cur, nxt):
        dst[...] = src[...]


def _ssd(xbc, dt, z, alog, expand, dskip, ng, bsz, seq):
    t = bsz * seq
    nc = seq // SSM_CHUNK
    row = lambda w: pl.BlockSpec((SSM_CHUNK, w), lambda b, c: (b * nc + c, 0))
    nxt = pl.BlockSpec((SSM_CHUNK, LANES), lambda b, c: (b * nc + jnp.minimum(c + 1, nc - 1), 0))
    decay_scratch = [pltpu.VMEM((3 * SSM_CHUNK, LANES), BF16), pltpu.VMEM((SSM_CHUNK, LANES), F32),
                     pltpu.VMEM((LANES, SSM_CHUNK), F32)]
    return pl.pallas_call(
        _ssd_kernel,
        grid=(bsz, nc),
        in_specs=[row(XBC_DIM), row(LANES), nxt, row(SSM_INNER), _const_spec(alog.shape),
                  _const_spec(expand.shape), _const_spec(dskip.shape), _const_spec(ng.shape)],
        out_specs=row(SSM_INNER),
        out_shape=jax.ShapeDtypeStruct((t, SSM_INNER), BF16),
        scratch_shapes=[pltpu.VMEM((SSM_GROUPS, SSM_STATE, SSM_INNER // SSM_GROUPS), F32),
                        pltpu.VMEM((SSM_CHUNK, SSM_INNER), F32),
                        pltpu.VMEM((SSM_CHUNK, SSM_INNER), BF16),
                        pltpu.VMEM((3 * SSM_CHUNK, SSM_INNER), F32)] + decay_scratch * 2,
        compiler_params=pltpu.CompilerParams(dimension_semantics=("arbitrary", "arbitrary"),
                                             vmem_limit_bytes=VMEM_LIMIT),
        name="ssd",
    )(xbc, dt, dt, z, alog, expand, dskip, ng)


def _tail_kernel(tiles_per_batch,
                 x_ref, attn_ref, ssm_ref, p_ref, woa_ref, wos_ref, fg_ref, wup_ref, fcw_ref, fcb_ref, wdn_ref,
                 pg_ref, wpg_ref, wpp_ref, out_ref, slabs, act_ref, carry, sbuf):
    i = pl.program_id(0)
    tm = x_ref.shape[0]
    n8 = tm // ROW_CLASSES
    nslab = D_MODEL // LANES

    @pl.when(i % tiles_per_batch == 0)
    def _():
        carry[...] = jnp.zeros_like(carry)

    attn = jnp.concatenate([attn_ref[s] for s in range(ATTN_SLABS)], axis=1).astype(BF16)
    x1 = x_ref[...] + _dot(ssm_ref[...], wos_ref[...]) + _dot(attn, woa_ref[...])

    h2 = _rms(x1) * fg_ref[...]
    for s in range(nslab):
        slabs[s] = h2[:, s * LANES:(s + 1) * LANES]
    h2p = jnp.concatenate(
        [jnp.concatenate([slabs[s, pl.ds(q, n8, stride=ROW_CLASSES), :] for q in range(ROW_CLASSES)], axis=0)
         for s in range(nslab)], axis=1).astype(BF16)

    def conv(u, cs, slot):
        blk = [u[q * n8:(q + 1) * n8] for q in range(ROW_CLASSES)]
        shifted = []
        for k in range(FFN_CONV - 1):
            sb = sbuf.at[slot * (FFN_CONV - 1) + k]
            sb[0:SUBLANES, :] = carry[k, :, cs]
            sb[SUBLANES:SUBLANES + n8, :] = blk[ROW_CLASSES - (FFN_CONV - 1) + k]
            shifted.append(sb[SUBLANES - 1:SUBLANES - 1 + n8, :])
            carry[k, :, cs] = sb[n8:n8 + SUBLANES, :]
        prev = shifted + blk
        w = [fcw_ref[k:k + 1, cs] for k in range(FFN_CONV)]
        b = fcb_ref[:, cs]
        return jnp.concatenate([b + w[2] * prev[q + 2] + w[1] * prev[q + 1] + w[0] * prev[q]
                                for q in range(ROW_CLASSES)], axis=0)

    for c in range(D_FF // FF_CHUNK):
        gs = slice(c * FF_CHUNK, (c + 1) * FF_CHUNK)
        vs = slice(D_FF + c * FF_CHUNK, D_FF + (c + 1) * FF_CHUNK)
        gate = conv(_dot(h2p, wup_ref[:, gs]), gs, 0)
        val = conv(_dot(h2p, wup_ref[:, vs]), vs, 1)
        act_ref[:, gs] = (_silu(gate) * val).astype(BF16)
    ffn_p = _dot(act_ref[...], wdn_ref[...])

    for s in range(nslab):
        for q in range(ROW_CLASSES):
            slabs[s, pl.ds(q, n8, stride=ROW_CLASSES), :] = ffn_p[q * n8:(q + 1) * n8, s * LANES:(s + 1) * LANES]
    x2 = x1 + jnp.concatenate([slabs[s] for s in range(nslab)], axis=1)

    h3 = (_rms(x2) * pg_ref[...]).astype(BF16)
    gate = jax.nn.sigmoid(_dot(h3, wpg_ref[...]))
    out_ref[...] = x2 + gate * _dot(p_ref[...].astype(BF16), wpp_ref[...])


def _tail(x2d, attn, ssm, p2d, woa, wos, fg, wup, fcw, fcb, wdn, pg, wpg, wpp, seq):
    t = x2d.shape[0]
    tm = TAIL_TM
    assert seq % tm == 0 and tm % (ROW_CLASSES * SUBLANES) == 0 and D_FF % FF_CHUNK == 0
    row = lambda w: pl.BlockSpec((tm, w), lambda i: (i, 0))
    consts = [woa, wos, fg, wup, fcw, fcb, wdn, pg, wpg, wpp]
    return pl.pallas_call(
        functools.partial(_tail_kernel, seq // tm),
        grid=(t // tm,),
        in_specs=[row(D_MODEL), pl.BlockSpec((ATTN_SLABS, tm, LANES), lambda i: (0, i, 0)),
                  row(SSM_INNER), row(PLE_DIM)] + [_const_spec(w.shape) for w in consts],
        out_specs=row(D_MODEL),
        out_shape=jax.ShapeDtypeStruct((t, D_MODEL), F32),
        scratch_shapes=[pltpu.VMEM((D_MODEL // LANES, tm, LANES), F32),
                        pltpu.VMEM((tm, D_FF), BF16),
                        pltpu.VMEM((FFN_CONV - 1, SUBLANES, 2 * D_FF), F32),
                        pltpu.VMEM((2 * (FFN_CONV - 1), SUBLANES + tm // ROW_CLASSES, FF_CHUNK), F32)],
        compiler_params=pltpu.CompilerParams(dimension_semantics=("arbitrary",),
                                             vmem_limit_bytes=VMEM_LIMIT),
        name="tail",
    )(x2d, attn, ssm, p2d, *consts)


def _q_perm():
    idx = np.arange(ATTN_DIM).reshape(ATTN_KV_HEADS, ATTN_GROUP, HEAD_DIM)
    return np.transpose(idx, (1, 0, 2)).reshape(-1)


def _layer(x, p, attn_norm_g, w_in, q_norm_g, k_norm_g, ssm_conv_w, ssm_conv_b, dt_bias, a_log,
           d_skip, ssm_norm_g, w_out, ffn_norm_g, w_up, ffn_conv_w, ffn_conv_b, w_down,
           ple_norm_g, w_ple_gate, w_ple_proj):
    bsz, seq, _ = x.shape
    t = bsz * seq
    x2d = x.reshape(t, D_MODEL)
    p2d = p.reshape(t, PLE_DIM)
    qperm = _q_perm()

    o_k, o_v, o_z = ATTN_DIM, ATTN_DIM + KV_DIM, ATTN_DIM + 2 * KV_DIM
    o_xbc, o_dt = o_z + SSM_INNER, o_z + SSM_INNER + XBC_DIM
    wqk = jnp.concatenate([w_in[:, :ATTN_DIM][:, qperm], w_in[:, o_k:o_v]], axis=1).astype(BF16)
    wv = w_in[:, o_v:o_z].astype(BF16)
    wz = w_in[:, o_z:o_xbc].astype(BF16)
    wxbc = w_in[:, o_xbc:o_dt].astype(BF16)
    head_reps = LANES // SSM_HEADS
    wdt = jnp.tile(w_in[:, o_dt:], (1, head_reps)).astype(BF16)
    qkg = jnp.concatenate([jnp.tile(q_norm_g * (HEAD_DIM ** -0.5 * LOG2E), ATTN_Q_HEADS),
                           jnp.tile(k_norm_g, ATTN_KV_HEADS)]).reshape(1, QK_DIM)
    hid = np.arange(256) // HEAD_DIM
    bd = jnp.asarray(hid[:, None] == hid[None, :], BF16)
    dtb = jnp.tile(dt_bias, head_reps).reshape(1, LANES)
    alog = jnp.tile(a_log, head_reps).reshape(1, LANES)

    qkv, z, xbc, dt = _in_proj(x2d, attn_norm_g.reshape(1, D_MODEL), wqk, wv, wz, wxbc, wdt, qkg, bd,
                               ssm_conv_w, ssm_conv_b.reshape(1, XBC_DIM), dtb, seq)

    attn = _attention(qkv, jnp.asarray(_attn_bias()), bsz, seq)

    src_lane = np.arange(LANES)[:, None]
    expand = jnp.asarray((src_lane % SSM_HEADS == (np.arange(SSM_INNER) // SSM_HEAD_DIM)[None, :])
                         & (src_lane < 3 * SSM_HEADS), BF16)
    dskip = jnp.repeat(d_skip, SSM_HEAD_DIM).reshape(1, SSM_INNER)
    ssm = _ssd(xbc, dt, z, alog, expand, dskip, ssm_norm_g.reshape(1, SSM_INNER), bsz, seq)

    woa = w_out[:ATTN_DIM][qperm].astype(BF16)
    wos = w_out[ATTN_DIM:].astype(BF16)
    out = _tail(x2d, attn, ssm, p2d, woa, wos, ffn_norm_g.reshape(1, D_MODEL), w_up.astype(BF16),
                ffn_conv_w, ffn_conv_b.reshape(1, 2 * D_FF), w_down.astype(BF16),
                ple_norm_g.reshape(1, D_MODEL), w_ple_gate.astype(BF16), w_ple_proj.astype(BF16), seq)
    return out.reshape(bsz, seq, D_MODEL)


def kernel(x, p, attn_norm_g, w_in, q_norm_g, k_norm_g, ssm_conv_w, ssm_conv_b, dt_bias, a_log, d_skip,
           ssm_norm_g, w_out, ffn_norm_g, w_up, ffn_conv_w, ffn_conv_b, w_down, ple_norm_g, w_ple_gate,
           w_ple_proj):
    depth = w_in.shape[0]
    for i in range(depth):
        x = _layer(x, p[i], attn_norm_g[i], w_in[i], q_norm_g[i], k_norm_g[i], ssm_conv_w[i], ssm_conv_b[i],
                   dt_bias[i], a_log[i], d_skip[i], ssm_norm_g[i], w_out[i], ffn_norm_g[i], w_up[i],
                   ffn_conv_w[i], ffn_conv_b[i], w_down[i], ple_norm_g[i], w_ple_gate[i], w_ple_proj[i])
    return x
```

```python
import functools

import jax
import jax.numpy as jnp
import numpy as np
from jax import lax
from jax.experimental import pallas as pl
from jax.experimental.pallas import tpu as pltpu

F32 = jnp.float32
BF16 = jnp.bfloat16

D_MODEL = 1024
HEAD_DIM = 64
ATTN_Q_HEADS = 8
ATTN_KV_HEADS = 4
ATTN_GROUP = ATTN_Q_HEADS // ATTN_KV_HEADS
DILATED_PATTERNS = ((128, 1), (512, 4), (2048, 16))
ATTN_BLOCK = 128
ATTN_DIM = ATTN_Q_HEADS * HEAD_DIM
KV_DIM = ATTN_KV_HEADS * HEAD_DIM
SSM_HEADS = 16
SSM_HEAD_DIM = 64
SSM_INNER = SSM_HEADS * SSM_HEAD_DIM
SSM_GROUPS = 2
SSM_STATE = 128
SSM_CONV = 4
SSM_CHUNK = 128
BC_DIM = SSM_GROUPS * SSM_STATE
XBC_DIM = SSM_INNER + 2 * BC_DIM
D_FF = 2816
FFN_CONV = 3
PLE_DIM = 256
EPS = 1e-6
LOG2E = 1.4426950408889634

LANES = 128
SUBLANES = 8
QK_DIM = ATTN_DIM + KV_DIM
QKV_DIM = QK_DIM + KV_DIM
QKV_SLABS = QKV_DIM // LANES
ATTN_SLABS = ATTN_DIM // LANES
KV_PAIRS = ATTN_KV_HEADS // 2
VMEM_LIMIT = 56 * 1024 * 1024

MAX_DIL = 16
SUPER = MAX_DIL * ATTN_BLOCK
IN_TM = 512
XBC_CHUNK = 512
TAIL_TM = 512
ROW_CLASSES = 8
FF_CHUNK = 256

assert DILATED_PATTERNS == ((128, 1), (512, 4), (2048, 16))
assert all(w // d == ATTN_BLOCK for w, d in DILATED_PATTERNS)


def _dot(a, b):
    return jnp.dot(a, b, preferred_element_type=F32)


def _dot_nt(a, b):
    return lax.dot_general(a, b, (((1,), (1,)), ((), ())), preferred_element_type=F32)


def _dot_tn(a, b):
    return lax.dot_general(a, b, (((0,), (0,)), ((), ())), preferred_element_type=F32)


def _rms(x):
    return x * lax.rsqrt(jnp.mean(x * x, axis=-1, keepdims=True) + EPS)


def _silu(x):
    return x * jax.nn.sigmoid(x)


def _const_spec(shape):
    nd = len(shape)
    return pl.BlockSpec(shape, lambda *_: (0,) * nd, pipeline_mode=pl.Buffered(1))


def _in_proj_kernel(tiles_per_batch,
                    x_ref, g_ref, wqk_ref, wv_ref, wz_ref, wxbc_ref, wdt_ref, qkg_ref, bd_ref,
                    cw_ref, cb_ref, dtb_ref,
                    qkv_ref, z_ref, xbc_ref, dt_ref, nat, mod4, *cbufs):
    i = pl.program_id(0)
    tm = x_ref.shape[0]

    @pl.when(i % tiles_per_batch == 0)
    def _():
        for cbuf in cbufs:
            cbuf[0:SUBLANES, :] = jnp.zeros((SUBLANES, XBC_CHUNK), F32)

    hb = (_rms(x_ref[...]) * g_ref[...]).astype(BF16)

    def ssm_conv(c):
        cbuf = cbufs[c]
        cs = slice(c * XBC_CHUNK, (c + 1) * XBC_CHUNK)
        cbuf[SUBLANES:SUBLANES + tm, :] = _dot(hb, wxbc_ref[:, cs])
        acc = cb_ref[:, cs] + cw_ref[SSM_CONV - 1:SSM_CONV, cs] * cbuf[SUBLANES:SUBLANES + tm, :]
        for kk in range(SSM_CONV - 1):
            off = SUBLANES - (SSM_CONV - 1) + kk
            acc = acc + cw_ref[kk:kk + 1, cs] * cbuf[off:off + tm, :]
        xbc_ref[:, cs] = _silu(acc)
        cbuf[0:SUBLANES, :] = cbuf[tm:tm + SUBLANES, :]

    def qk_norm():
        qk = _dot(hb, wqk_ref[...])
        sq = qk * qk
        sq_hi = sq.astype(BF16)
        sq_lo = (sq - sq_hi.astype(F32)).astype(BF16)
        bd = bd_ref[...]
        for c in range(QK_DIM // 256):
            sl = slice(c * 256, (c + 1) * 256)
            ss = _dot(sq_hi[:, sl], bd) + _dot(sq_lo[:, sl], bd)
            qkn = qk[:, sl] * lax.rsqrt(ss * (1.0 / HEAD_DIM) + EPS) * qkg_ref[:, sl]
            nat[2 * c] = qkn[:, :LANES]
            nat[2 * c + 1] = qkn[:, LANES:]

    def v_proj():
        v = _dot(hb, wv_ref[...])
        nat[QKV_SLABS - 2] = v[:, :LANES]
        nat[QKV_SLABS - 1] = v[:, LANES:]

    def relayout():
        quarter = tm // 4
        sixteenth = tm // MAX_DIL
        for s in range(QKV_SLABS):
            for a in range(4):
                mod4[s, a * quarter:(a + 1) * quarter, :] = nat[s, pl.ds(a, quarter, stride=4), :]
        k = i % (SUPER // tm)
        for s in range(QKV_SLABS):
            for a in range(4):
                for b in range(4):
                    dst = pl.multiple_of((4 * b + a) * ATTN_BLOCK + sixteenth * k, sixteenth)
                    qkv_ref[s, pl.ds(dst, sixteenth), :] = mod4[s, pl.ds(a * quarter + b, sixteenth, stride=4), :]

    def z_proj():
        z_ref[...] = _dot(hb, wz_ref[...]).astype(BF16)

    def dt_proj():
        dtr = _dot(hb, wdt_ref[...]) + dtb_ref[...]
        dt_ref[...] = jnp.maximum(dtr, 0.0) + jnp.log1p(jnp.exp(-jnp.abs(dtr)))

    for c in range(len(cbufs)):
        ssm_conv(c)
    qk_norm()
    v_proj()
    relayout()
    z_proj()
    dt_proj()


def _in_proj(x2d, g, wqk, wv, wz, wxbc, wdt, qkg, bd, cw, cb, dtb, seq):
    t = x2d.shape[0]
    tm = IN_TM
    assert SUPER % tm == 0 and tm % (4 * MAX_DIL) == 0 and seq % SUPER == 0
    grid = (t // tm,)
    row = lambda w: pl.BlockSpec((tm, w), lambda i: (i, 0))
    per_super = SUPER // tm
    return pl.pallas_call(
        functools.partial(_in_proj_kernel, seq // tm),
        grid=grid,
        in_specs=[row(D_MODEL), _const_spec(g.shape), _const_spec(wqk.shape), _const_spec(wv.shape),
                  _const_spec(wz.shape), _const_spec(wxbc.shape), _const_spec(wdt.shape),
                  _const_spec(qkg.shape), _const_spec(bd.shape), _const_spec(cw.shape),
                  _const_spec(cb.shape), _const_spec(dtb.shape)],
        out_specs=[pl.BlockSpec((QKV_SLABS, SUPER, LANES), lambda i: (0, i // per_super, 0)),
                   row(SSM_INNER), row(XBC_DIM), row(LANES)],
        out_shape=[jax.ShapeDtypeStruct((QKV_SLABS, t, LANES), F32), jax.ShapeDtypeStruct((t, SSM_INNER), BF16),
                   jax.ShapeDtypeStruct((t, XBC_DIM), F32), jax.ShapeDtypeStruct((t, LANES), F32)],
        scratch_shapes=[pltpu.VMEM((QKV_SLABS, tm, LANES), F32),
                        pltpu.VMEM((QKV_SLABS, tm, LANES), F32)]
                       + [pltpu.VMEM((tm + SUBLANES, XBC_CHUNK), F32)] * (XBC_DIM // XBC_CHUNK),
        compiler_params=pltpu.CompilerParams(dimension_semantics=("arbitrary",),
                                             vmem_limit_bytes=VMEM_LIMIT),
        name="in_proj",
    )(x2d, g, wqk, wv, wz, wxbc, wdt, qkg, bd, cw, cb, dtb)


def _block_perm(dil):
    reps = MAX_DIL // dil
    per = ATTN_BLOCK // reps
    pos = np.arange(ATTN_BLOCK)
    return (pos % per) * reps + pos // per


def _attn_bias():
    out = []
    for _, dil in DILATED_PATTERNS[::-1]:
        perm = _block_perm(dil)
        qi = perm[:, None]
        kj = np.concatenate([perm, ATTN_BLOCK + perm])[None, :]
        delta = ATTN_BLOCK + qi - kj
        band = (delta >= 0) & (delta <= ATTN_BLOCK)
        for first in (False, True):
            ok = band & (kj >= ATTN_BLOCK) if first else band
            out.append(np.tile(np.where(ok, 0.0, -np.inf).astype(np.float32), (4, 1)))
    return np.stack(out)


def _attn_kernel(cur_ref, prev_ref, bias_ref, out_ref, acc_o, acc_m, acc_l):
    first = (pl.program_id(1) == 0).astype(jnp.int32)
    lane = lax.broadcasted_iota(jnp.int32, (ATTN_BLOCK, LANES), 1)
    lo_half = lane < HEAD_DIM
    ones = jnp.ones((2 * ATTN_BLOCK, LANES), BF16)
    k_slab, v_slab = ATTN_SLABS, ATTN_SLABS + KV_PAIRS

    def rows(ref, slab, pieces):
        parts = [ref[slab, pl.ds(st, sz), :] for st, sz in pieces]
        return parts[0] if len(parts) == 1 else jnp.concatenate(parts, axis=0)

    def unit(q_pieces, prev_pieces, prev_in_cur, bias_idx, mode, out_start=None):
        bias = bias_ref[bias_idx]
        for p in range(KV_PAIRS):
            if prev_in_cur:
                kp, vp = rows(cur_ref, k_slab + p, prev_pieces), rows(cur_ref, v_slab + p, prev_pieces)
            else:
                kp, vp = rows(prev_ref, p, prev_pieces), rows(prev_ref, KV_PAIRS + p, prev_pieces)
            k2 = jnp.concatenate([kp, rows(cur_ref, k_slab + p, q_pieces)], axis=0).astype(BF16)
            v2 = jnp.concatenate([vp, rows(cur_ref, v_slab + p, q_pieces)], axis=0).astype(BF16)
            v2 = jnp.concatenate([v2, ones], axis=1)
            qs = []
            for g in range(ATTN_GROUP):
                q = rows(cur_ref, g * KV_PAIRS + p, q_pieces)
                qs += [jnp.where(lo_half, q, 0.0), jnp.where(lo_half, 0.0, q)]
            q4 = jnp.concatenate(qs, axis=0).astype(BF16)
            s = _dot_nt(q4, k2) + bias
            m = jnp.max(s, axis=-1, keepdims=True)
            pr = jnp.exp2(s - m).astype(BF16)
            ol = _dot(pr, v2)
            for g in range(ATTN_GROUP):
                r0, r1 = (2 * g) * ATTN_BLOCK, (2 * g + 1) * ATTN_BLOCK
                o = jnp.where(lo_half, ol[r0:r0 + ATTN_BLOCK, :LANES], ol[r1:r1 + ATTN_BLOCK, :LANES])
                l = jnp.where(lo_half, ol[r0:r0 + ATTN_BLOCK, LANES:], ol[r1:r1 + ATTN_BLOCK, LANES:])
                mm = jnp.where(lo_half, m[r0:r0 + ATTN_BLOCK], m[r1:r1 + ATTN_BLOCK])
                slab = g * KV_PAIRS + p
                if mode != "init":
                    m_old = rows(acc_m, slab, q_pieces)
                    m_new = jnp.maximum(m_old, mm)
                    w_old, w_new = jnp.exp2(m_old - m_new), jnp.exp2(mm - m_new)
                    o = rows(acc_o, slab, q_pieces) * w_old + o * w_new
                    l = rows(acc_l, slab, q_pieces) * w_old + l * w_new
                    mm = m_new
                if mode == "final":
                    res = o / l
                    for r in range(MAX_DIL):
                        out_ref[slab, pl.ds(out_start + r, SUBLANES, stride=MAX_DIL), :] = (
                            res[r * SUBLANES:(r + 1) * SUBLANES])
                else:
                    off = 0
                    for st, sz in q_pieces:
                        acc_o[slab, pl.ds(st, sz), :] = o[off:off + sz]
                        acc_m[slab, pl.ds(st, sz), :] = mm[off:off + sz]
                        acc_l[slab, pl.ds(st, sz), :] = l[off:off + sz]
                        off += sz

    def aligned(x, m):
        return x if isinstance(x, int) else pl.multiple_of(x, m)

    def d16_body(r, carry):
        st = pl.multiple_of(r * ATTN_BLOCK, ATTN_BLOCK)
        unit([(st, ATTN_BLOCK)], [(st, ATTN_BLOCK)], False, first, "init")
        return carry
    lax.fori_loop(0, MAX_DIL, d16_body, 0, unroll=MAX_DIL)

    def d4_pieces(c, row0):
        return [(aligned((4 * a + c) * ATTN_BLOCK + row0, 32), 32) for a in range(4)]

    def d4_body(c, carry):
        unit(d4_pieces(c, 0), d4_pieces(c, ATTN_BLOCK - 32), False, 2 + first, "merge")

        for n in range(1, 4):
            unit(d4_pieces(c, 32 * n), d4_pieces(c, 32 * (n - 1)), True, 2, "merge")
        return carry
    lax.fori_loop(0, 4, d4_body, 0, unroll=2)

    def d1_pieces(row0):
        return [(aligned(r * ATTN_BLOCK + row0, SUBLANES), SUBLANES) for r in range(MAX_DIL)]

    unit(d1_pieces(0), d1_pieces(ATTN_BLOCK - SUBLANES), False, 4 + first, "final", 0)

    def d1_body(n, carry):
        unit(d1_pieces(SUBLANES * n), d1_pieces(SUBLANES * (n - 1)), True, 4, "final", n * ATTN_BLOCK)
        return carry
    lax.fori_loop(1, SUPER // ATTN_BLOCK, d1_body, 0, unroll=5)


def _attention(qkv, bias, bsz, seq):
    t = bsz * seq
    nsb = seq // SUPER
    acc = pltpu.VMEM((ATTN_SLABS, SUPER, LANES), F32)
    return pl.pallas_call(
        _attn_kernel,
        grid=(bsz, nsb),
        in_specs=[pl.BlockSpec((QKV_SLABS, SUPER, LANES), lambda b, s: (0, b * nsb + s, 0)),
                  pl.BlockSpec((QKV_SLABS // 2, SUPER, LANES),
                               lambda b, s: (1, b * nsb + jnp.maximum(s - 1, 0), 0)),
                  _const_spec(bias.shape)],
        out_specs=pl.BlockSpec((ATTN_SLABS, SUPER, LANES), lambda b, s: (0, b * nsb + s, 0)),
        out_shape=jax.ShapeDtypeStruct((ATTN_SLABS, t, LANES), F32),
        scratch_shapes=[acc, acc, acc],
        compiler_params=pltpu.CompilerParams(dimension_semantics=("arbitrary", "arbitrary"),
                                             vmem_limit_bytes=VMEM_LIMIT),
        name="attention",
    )(qkv, qkv, bias)


def _ssd_kernel(xbc_ref, dt_ref, dtn_ref, z_ref, alog_ref, expand_ref, dskip_ref, ng_ref, out_ref,
                state_ref, y_ref, xde_ref, e_ref, *decay_refs):
    cur, nxt = decay_refs[:3], decay_refs[3:]
    c = pl.program_id(1)
    ln = SSM_CHUNK

    lane = lax.broadcasted_iota(jnp.int32, (ln, LANES), 1)
    lo_half = lane < SSM_HEAD_DIM
    part = lane // SSM_HEADS
    row = lax.broadcasted_iota(jnp.int32, (ln, ln), 0)
    col = lax.broadcasted_iota(jnp.int32, (ln, ln), 1)
    causal = row >= col
    tril = causal.astype(BF16)

    def split3(v):
        hi = v.astype(BF16).astype(F32)
        mid = (v - hi).astype(BF16).astype(F32)
        lo = (v - hi) - mid
        return hi, mid, lo

    def pack3(v):
        hi, mid, lo = split3(v)
        return jnp.where(part == 0, hi, jnp.where(part == 1, mid, jnp.where(part == 2, lo, 0.0))).astype(BF16)

    def decay_chain(dt, dst):
        a = dt * (-LOG2E * jnp.exp(alog_ref[...]))
        a3 = jnp.concatenate(split3(a), axis=0).astype(BF16)
        acum = _dot(jnp.concatenate([tril, tril, tril], axis=1), a3)
        total = acum[ln - 1:ln, :]
        packed = jnp.concatenate([pack3(dt), pack3(jnp.exp2(acum)), pack3(jnp.exp2(total - acum))], axis=0)
        packed_ref, acum_ref, acumt_ref = dst
        packed_ref[...] = packed
        acum_ref[...] = acum
        acumt_ref[...] = acum.T

    @pl.when(c == 0)
    def _():
        state_ref[...] = jnp.zeros_like(state_ref)
        decay_chain(dt_ref[...], cur)

    decay_chain(dtn_ref[...], nxt)

    bgs, cbs, y_offs = [], [], []
    for g in range(SSM_GROUPS):
        bg = xbc_ref[:, SSM_INNER + g * SSM_STATE:SSM_INNER + (g + 1) * SSM_STATE].astype(BF16)
        cg = xbc_ref[:, SSM_INNER + BC_DIM + g * SSM_STATE:SSM_INNER + BC_DIM + (g + 1) * SSM_STATE].astype(BF16)
        bgs.append(bg)
        cbs.append(_dot_nt(cg, bg))
        y_offs.append(_dot(cg, state_ref[g].astype(BF16)))

    e_ref[...] = _dot(cur[0][...], expand_ref[...])
    acum = cur[1][...]
    acum_t = cur[2][...]
    gw = SSM_INNER // SSM_GROUPS
    pairs = SSM_HEADS // SSM_GROUPS // 2
    ssq = jnp.zeros((ln, LANES), F32)
    for g in range(SSM_GROUPS):
        for j in range(pairs):
            h0 = g * 2 * pairs + 2 * j
            ps = slice(h0 * SSM_HEAD_DIM, (h0 + 2) * SSM_HEAD_DIM)
            xs = xbc_ref[:, ps]
            xdt = xs * e_ref[0:ln, ps]
            ms = []
            for h in (h0, h0 + 1):
                seg = acum[:, h:h + 1] - acum_t[h:h + 1, :]
                ms.append(cbs[g] * jnp.exp2(jnp.where(causal, seg, -jnp.inf)))
            mm = jnp.concatenate(ms, axis=1).astype(BF16)
            rhs = jnp.concatenate([jnp.where(lo_half, xdt, 0.0), jnp.where(lo_half, 0.0, xdt)],
                                  axis=0).astype(BF16)
            y = (_dot(mm, rhs) + y_offs[g][:, j * LANES:(j + 1) * LANES] * e_ref[ln:2 * ln, ps]
                 + dskip_ref[:, ps] * xs)
            y = y * _silu(z_ref[:, ps].astype(F32))
            y_ref[:, ps] = y
            ssq = ssq + y * y
            xde_ref[:, ps] = (xdt * e_ref[2 * ln:3 * ln, ps]).astype(BF16)
        gs = slice(g * gw, (g + 1) * gw)
        state_ref[g] = state_ref[g] * e_ref[2 * ln - 1:2 * ln, gs] + _dot_tn(bgs[g], xde_ref[:, gs])

    scale = lax.rsqrt(jnp.sum(ssq, axis=-1, keepdims=True) * (1.0 / SSM_INNER) + EPS)
    out_ref[...] = (y_ref[...] * scale * ng_ref[...]).astype(BF16)
    for dst, src in zip(cur, nxt):
        dst[...] = src[...]


def _ssd(xbc, dt, z, alog, expand, dskip, ng, bsz, seq):
    t = bsz * seq
    nc = seq // SSM_CHUNK
    row = lambda w: pl.BlockSpec((SSM_CHUNK, w), lambda b, c: (b * nc + c, 0))
    nxt = pl.BlockSpec((SSM_CHUNK, LANES), lambda b, c: (b * nc + jnp.minimum(c + 1, nc - 1), 0))
    decay_scratch = [pltpu.VMEM((3 * SSM_CHUNK, LANES), BF16), pltpu.VMEM((SSM_CHUNK, LANES), F32),
                     pltpu.VMEM((LANES, SSM_CHUNK), F32)]
    return pl.pallas_call(
        _ssd_kernel,
        grid=(bsz, nc),
        in_specs=[row(XBC_DIM), row(LANES), nxt, row(SSM_INNER), _const_spec(alog.shape),
                  _const_spec(expand.shape), _const_spec(dskip.shape), _const_spec(ng.shape)],
        out_specs=row(SSM_INNER),
        out_shape=jax.ShapeDtypeStruct((t, SSM_INNER), BF16),
        scratch_shapes=[pltpu.VMEM((SSM_GROUPS, SSM_STATE, SSM_INNER // SSM_GROUPS), F32),
                        pltpu.VMEM((SSM_CHUNK, SSM_INNER), F32),
                        pltpu.VMEM((SSM_CHUNK, SSM_INNER), BF16),
                        pltpu.VMEM((3 * SSM_CHUNK, SSM_INNER), F32)] + decay_scratch * 2,
        compiler_params=pltpu.CompilerParams(dimension_semantics=("arbitrary", "arbitrary"),
                                             vmem_limit_bytes=VMEM_LIMIT),
        name="ssd",
    )(xbc, dt, dt, z, alog, expand, dskip, ng)


def _tail_kernel(tiles_per_batch,
                 x_ref, attn_ref, ssm_ref, p_ref, woa_ref, wos_ref, fg_ref, wup_ref, fcw_ref, fcb_ref, wdn_ref,
                 pg_ref, wpg_ref, wpp_ref, out_ref, slabs, act_ref, carry, sbuf):
    i = pl.program_id(0)
    tm = x_ref.shape[0]
    n8 = tm // ROW_CLASSES
    nslab = D_MODEL // LANES

    @pl.when(i % tiles_per_batch == 0)
    def _():
        carry[...] = jnp.zeros_like(carry)

    attn = jnp.concatenate([attn_ref[s] for s in range(ATTN_SLABS)], axis=1).astype(BF16)
    x1 = x_ref[...] + _dot(ssm_ref[...], wos_ref[...]) + _dot(attn, woa_ref[...])

    h2 = _rms(x1) * fg_ref[...]
    for s in range(nslab):
        slabs[s] = h2[:, s * LANES:(s + 1) * LANES]
    h2p = jnp.concatenate(
        [jnp.concatenate([slabs[s, pl.ds(q, n8, stride=ROW_CLASSES), :] for q in range(ROW_CLASSES)], axis=0)
         for s in range(nslab)], axis=1).astype(BF16)

    def conv(u, cs, slot):
        blk = [u[q * n8:(q + 1) * n8] for q in range(ROW_CLASSES)]
        shifted = []
        for k in range(FFN_CONV - 1):
            sb = sbuf.at[slot * (FFN_CONV - 1) + k]
            sb[0:SUBLANES, :] = carry[k, :, cs]
            sb[SUBLANES:SUBLANES + n8, :] = blk[ROW_CLASSES - (FFN_CONV - 1) + k]
            shifted.append(sb[SUBLANES - 1:SUBLANES - 1 + n8, :])
            carry[k, :, cs] = sb[n8:n8 + SUBLANES, :]
        prev = shifted + blk
        w = [fcw_ref[k:k + 1, cs] for k in range(FFN_CONV)]
        b = fcb_ref[:, cs]
        return jnp.concatenate([b + w[2] * prev[q + 2] + w[1] * prev[q + 1] + w[0] * prev[q]
                                for q in range(ROW_CLASSES)], axis=0)

    for c in range(D_FF // FF_CHUNK):
        gs = slice(c * FF_CHUNK, (c + 1) * FF_CHUNK)
        vs = slice(D_FF + c * FF_CHUNK, D_FF + (c + 1) * FF_CHUNK)
        gate = conv(_dot(h2p, wup_ref[:, gs]), gs, 0)
        val = conv(_dot(h2p, wup_ref[:, vs]), vs, 1)
        act_ref[:, gs] = (_silu(gate) * val).astype(BF16)
    ffn_p = _dot(act_ref[...], wdn_ref[...])

    for s in range(nslab):
        for q in range(ROW_CLASSES):
            slabs[s, pl.ds(q, n8, stride=ROW_CLASSES), :] = ffn_p[q * n8:(q + 1) * n8, s * LANES:(s + 1) * LANES]
    x2 = x1 + jnp.concatenate([slabs[s] for s in range(nslab)], axis=1)

    h3 = (_rms(x2) * pg_ref[...]).astype(BF16)
    gate = jax.nn.sigmoid(_dot(h3, wpg_ref[...]))
    out_ref[...] = x2 + gate * _dot(p_ref[...].astype(BF16), wpp_ref[...])


def _tail(x2d, attn, ssm, p2d, woa, wos, fg, wup, fcw, fcb, wdn, pg, wpg, wpp, seq):
    t = x2d.shape[0]
    tm = TAIL_TM
    assert seq % tm == 0 and tm % (ROW_CLASSES * SUBLANES) == 0 and D_FF % FF_CHUNK == 0
    row = lambda w: pl.BlockSpec((tm, w), lambda i: (i, 0))
    consts = [woa, wos, fg, wup, fcw, fcb, wdn, pg, wpg, wpp]
    return pl.pallas_call(
        functools.partial(_tail_kernel, seq // tm),
        grid=(t // tm,),
        in_specs=[row(D_MODEL), pl.BlockSpec((ATTN_SLABS, tm, LANES), lambda i: (0, i, 0)),
                  row(SSM_INNER), row(PLE_DIM)] + [_const_spec(w.shape) for w in consts],
        out_specs=row(D_MODEL),
        out_shape=jax.ShapeDtypeStruct((t, D_MODEL), F32),
        scratch_shapes=[pltpu.VMEM((D_MODEL // LANES, tm, LANES), F32),
                        pltpu.VMEM((tm, D_FF), BF16),
                        pltpu.VMEM((FFN_CONV - 1, SUBLANES, 2 * D_FF), F32),
                        pltpu.VMEM((2 * (FFN_CONV - 1), SUBLANES + tm // ROW_CLASSES, FF_CHUNK), F32)],
        compiler_params=pltpu.CompilerParams(dimension_semantics=("arbitrary",),
                                             vmem_limit_bytes=VMEM_LIMIT),
        name="tail",
    )(x2d, attn, ssm, p2d, *consts)


def _q_perm():
    idx = np.arange(ATTN_DIM).reshape(ATTN_KV_HEADS, ATTN_GROUP, HEAD_DIM)
    return np.transpose(idx, (1, 0, 2)).reshape(-1)


def _layer(x, p, attn_norm_g, w_in, q_norm_g, k_norm_g, ssm_conv_w, ssm_conv_b, dt_bias, a_log,
           d_skip, ssm_norm_g, w_out, ffn_norm_g, w_up, ffn_conv_w, ffn_conv_b, w_down,
           ple_norm_g, w_ple_gate, w_ple_proj):
    bsz, seq, _ = x.shape
    t = bsz * seq
    x2d = x.reshape(t, D_MODEL)
    p2d = p.reshape(t, PLE_DIM)
    qperm = _q_perm()

    o_k, o_v, o_z = ATTN_DIM, ATTN_DIM + KV_DIM, ATTN_DIM + 2 * KV_DIM
    o_xbc, o_dt = o_z + SSM_INNER, o_z + SSM_INNER + XBC_DIM
    wqk = jnp.concatenate([w_in[:, :ATTN_DIM][:, qperm], w_in[:, o_k:o_v]], axis=1).astype(BF16)
    wv = w_in[:, o_v:o_z].astype(BF16)
    wz = w_in[:, o_z:o_xbc].astype(BF16)
    wxbc = w_in[:, o_xbc:o_dt].astype(BF16)
    head_reps = LANES // SSM_HEADS
    wdt = jnp.tile(w_in[:, o_dt:], (1, head_reps)).astype(BF16)
    qkg = jnp.concatenate([jnp.tile(q_norm_g * (HEAD_DIM ** -0.5 * LOG2E), ATTN_Q_HEADS),
                           jnp.tile(k_norm_g, ATTN_KV_HEADS)]).reshape(1, QK_DIM)
    hid = np.arange(256) // HEAD_DIM
    bd = jnp.asarray(hid[:, None] == hid[None, :], BF16)
    dtb = jnp.tile(dt_bias, head_reps).reshape(1, LANES)
    alog = jnp.tile(a_log, head_reps).reshape(1, LANES)

    qkv, z, xbc, dt = _in_proj(x2d, attn_norm_g.reshape(1, D_MODEL), wqk, wv, wz, wxbc, wdt, qkg, bd,
                               ssm_conv_w, ssm_conv_b.reshape(1, XBC_DIM), dtb, seq)

    attn = _attention(qkv, jnp.asarray(_attn_bias()), bsz, seq)

    src_lane = np.arange(LANES)[:, None]
    expand = jnp.asarray((src_lane % SSM_HEADS == (np.arange(SSM_INNER) // SSM_HEAD_DIM)[None, :])
                         & (src_lane < 3 * SSM_HEADS), BF16)
    dskip = jnp.repeat(d_skip, SSM_HEAD_DIM).reshape(1, SSM_INNER)
    ssm = _ssd(xbc, dt, z, alog, expand, dskip, ssm_norm_g.reshape(1, SSM_INNER), bsz, seq)

    woa = w_out[:ATTN_DIM][qperm].astype(BF16)
    wos = w_out[ATTN_DIM:].astype(BF16)
    out = _tail(x2d, attn, ssm, p2d, woa, wos, ffn_norm_g.reshape(1, D_MODEL), w_up.astype(BF16),
                ffn_conv_w, ffn_conv_b.reshape(1, 2 * D_FF), w_down.astype(BF16),
                ple_norm_g.reshape(1, D_MODEL), w_ple_gate.astype(BF16), w_ple_proj.astype(BF16), seq)
    return out.reshape(bsz, seq, D_MODEL)


def kernel(x, p, attn_norm_g, w_in, q_norm_g, k_norm_g, ssm_conv_w, ssm_conv_b, dt_bias, a_log, d_skip,
           ssm_norm_g, w_out, ffn_norm_g, w_up, ffn_conv_w, ffn_conv_b, w_down, ple_norm_g, w_ple_gate,
           w_ple_proj):
    depth = w_in.shape[0]
    for i in range(depth):
        x = _layer(x, p[i], attn_norm_g[i], w_in[i], q_norm_g[i], k_norm_g[i], ssm_conv_w[i], ssm_conv_b[i],
                   dt_bias[i], a_log[i], d_skip[i], ssm_norm_g[i], w_out[i], ffn_norm_g[i], w_up[i],
                   ffn_conv_w[i], ffn_conv_b[i], w_down[i], ple_norm_g[i], w_ple_gate[i], w_ple_proj[i])
    return x
```

```python
import functools

import jax
import jax.numpy as jnp
import numpy as np
from jax import lax
from jax.experimental import pallas as pl
from jax.experimental.pallas import tpu as pltpu

F32 = jnp.float32
BF16 = jnp.bfloat16

D_MODEL = 1024
HEAD_DIM = 64
ATTN_Q_HEADS = 8
ATTN_KV_HEADS = 4
ATTN_GROUP = ATTN_Q_HEADS // ATTN_KV_HEADS
DILATED_PATTERNS = ((128, 1), (512, 4), (2048, 16))
ATTN_BLOCK = 128
ATTN_DIM = ATTN_Q_HEADS * HEAD_DIM
KV_DIM = ATTN_KV_HEADS * HEAD_DIM
SSM_HEADS = 16
SSM_HEAD_DIM = 64
SSM_INNER = SSM_HEADS * SSM_HEAD_DIM
SSM_GROUPS = 2
SSM_STATE = 128
SSM_CONV = 4
SSM_CHUNK = 128
BC_DIM = SSM_GROUPS * SSM_STATE
XBC_DIM = SSM_INNER + 2 * BC_DIM
D_FF = 2816
FFN_CONV = 3
PLE_DIM = 256
EPS = 1e-6
LOG2E = 1.4426950408889634

LANES = 128
SUBLANES = 8
QK_DIM = ATTN_DIM + KV_DIM
QKV_DIM = QK_DIM + KV_DIM
QKV_SLABS = QKV_DIM // LANES
ATTN_SLABS = ATTN_DIM // LANES
KV_PAIRS = ATTN_KV_HEADS // 2
VMEM_LIMIT = 56 * 1024 * 1024

MAX_DIL = 16
SUPER = MAX_DIL * ATTN_BLOCK
IN_TM = 512
XBC_CHUNK = 512
TAIL_TM = 512
ROW_CLASSES = 8
FF_CHUNK = 256

assert DILATED_PATTERNS == ((128, 1), (512, 4), (2048, 16))
assert all(w // d == ATTN_BLOCK for w, d in DILATED_PATTERNS)


def _dot(a, b):
    return jnp.dot(a, b, preferred_element_type=F32)


def _dot_nt(a, b):
    return lax.dot_general(a, b, (((1,), (1,)), ((), ())), preferred_element_type=F32)


def _dot_tn(a, b):
    return lax.dot_general(a, b, (((0,), (0,)), ((), ())), preferred_element_type=F32)


def _rms(x):
    return x * lax.rsqrt(jnp.mean(x * x, axis=-1, keepdims=True) + EPS)


def _silu(x):
    return x * jax.nn.sigmoid(x)


def _const_spec(shape):
    nd = len(shape)
    return pl.BlockSpec(shape, lambda *_: (0,) * nd, pipeline_mode=pl.Buffered(1))


def _in_proj_kernel(tiles_per_batch,
                    x_ref, g_ref, wqk_ref, wv_ref, wz_ref, wxbc_ref, wdt_ref, qkg_ref, bd_ref,
                    cw_ref, cb_ref, dtb_ref,
                    qkv_ref, z_ref, xbc_ref, dt_ref, nat, mod4, *cbufs):
    i = pl.program_id(0)
    tm = x_ref.shape[0]

    @pl.when(i % tiles_per_batch == 0)
    def _():
        for cbuf in cbufs:
            cbuf[0:SUBLANES, :] = jnp.zeros((SUBLANES, XBC_CHUNK), F32)

    hb = (_rms(x_ref[...]) * g_ref[...]).astype(BF16)

    def ssm_conv(c):
        cbuf = cbufs[c]
        cs = slice(c * XBC_CHUNK, (c + 1) * XBC_CHUNK)
        cbuf[SUBLANES:SUBLANES + tm, :] = _dot(hb, wxbc_ref[:, cs])
        acc = cb_ref[:, cs] + cw_ref[SSM_CONV - 1:SSM_CONV, cs] * cbuf[SUBLANES:SUBLANES + tm, :]
        for kk in range(SSM_CONV - 1):
            off = SUBLANES - (SSM_CONV - 1) + kk
            acc = acc + cw_ref[kk:kk + 1, cs] * cbuf[off:off + tm, :]
        xbc_ref[:, cs] = _silu(acc)
        cbuf[0:SUBLANES, :] = cbuf[tm:tm + SUBLANES, :]

    def qk_norm():
        qk = _dot(hb, wqk_ref[...])
        sq = qk * qk
        sq_hi = sq.astype(BF16)
        sq_lo = (sq - sq_hi.astype(F32)).astype(BF16)
        bd = bd_ref[...]
        for c in range(QK_DIM // 256):
            sl = slice(c * 256, (c + 1) * 256)
            ss = _dot(sq_hi[:, sl], bd) + _dot(sq_lo[:, sl], bd)
            qkn = qk[:, sl] * lax.rsqrt(ss * (1.0 / HEAD_DIM) + EPS) * qkg_ref[:, sl]
            nat[2 * c] = qkn[:, :LANES]
            nat[2 * c + 1] = qkn[:, LANES:]

    def v_proj():
        v = _dot(hb, wv_ref[...])
        nat[QKV_SLABS - 2] = v[:, :LANES]
        nat[QKV_SLABS - 1] = v[:, LANES:]

    def relayout():
        quarter = tm // 4
        sixteenth = tm // MAX_DIL
        for s in range(QKV_SLABS):
            for a in range(4):
                mod4[s, a * quarter:(a + 1) * quarter, :] = nat[s, pl.ds(a, quarter, stride=4), :]
        for s in range(QKV_SLABS):
            for a in range(4):
                for b in range(4):
                    qkv_ref[s, 0, 4 * b + a, 0] = mod4[s, pl.ds(a * quarter + b, sixteenth, stride=4), :]

    def z_proj():
        z_ref[...] = _dot(hb, wz_ref[...]).astype(BF16)

    def dt_proj():
        dtr = _dot(hb, wdt_ref[...]) + dtb_ref[...]
        dt_ref[...] = jnp.maximum(dtr, 0.0) + jnp.log1p(jnp.exp(-jnp.abs(dtr)))

    for c in range(len(cbufs)):
        ssm_conv(c)
    qk_norm()
    v_proj()
    relayout()
    z_proj()
    dt_proj()


def _in_proj(x2d, g, wqk, wv, wz, wxbc, wdt, qkg, bd, cw, cb, dtb, seq):
    t = x2d.shape[0]
    tm = IN_TM
    assert SUPER % tm == 0 and tm % (4 * MAX_DIL) == 0 and seq % SUPER == 0
    grid = (t // tm,)
    row = lambda w: pl.BlockSpec((tm, w), lambda i: (i, 0))
    per_super = SUPER // tm
    qkv_shape = (QKV_SLABS, t // SUPER, MAX_DIL, per_super, tm // MAX_DIL, LANES)
    qkv, z, xbc, dt = pl.pallas_call(
        functools.partial(_in_proj_kernel, seq // tm),
        grid=grid,
        in_specs=[row(D_MODEL), _const_spec(g.shape), _const_spec(wqk.shape), _const_spec(wv.shape),
                  _const_spec(wz.shape), _const_spec(wxbc.shape), _const_spec(wdt.shape),
                  _const_spec(qkg.shape), _const_spec(bd.shape), _const_spec(cw.shape),
                  _const_spec(cb.shape), _const_spec(dtb.shape)],
        out_specs=[pl.BlockSpec((QKV_SLABS, 1, MAX_DIL, 1, tm // MAX_DIL, LANES),
                                lambda i: (0, i // per_super, 0, i % per_super, 0, 0)),
                   row(SSM_INNER), row(XBC_DIM), row(LANES)],
        out_shape=[jax.ShapeDtypeStruct(qkv_shape, F32), jax.ShapeDtypeStruct((t, SSM_INNER), BF16),
                   jax.ShapeDtypeStruct((t, XBC_DIM), F32), jax.ShapeDtypeStruct((t, LANES), F32)],
        scratch_shapes=[pltpu.VMEM((QKV_SLABS, tm, LANES), F32),
                        pltpu.VMEM((QKV_SLABS, tm, LANES), F32)]
                       + [pltpu.VMEM((tm + SUBLANES, XBC_CHUNK), F32)] * (XBC_DIM // XBC_CHUNK),
        compiler_params=pltpu.CompilerParams(dimension_semantics=("arbitrary",),
                                             vmem_limit_bytes=VMEM_LIMIT),
        name="in_proj",
    )(x2d, g, wqk, wv, wz, wxbc, wdt, qkg, bd, cw, cb, dtb)
    return qkv.reshape(QKV_SLABS, t, LANES), z, xbc, dt


def _block_perm(dil):
    reps = MAX_DIL // dil
    per = ATTN_BLOCK // reps
    pos = np.arange(ATTN_BLOCK)
    return (pos % per) * reps + pos // per


def _attn_bias():
    out = []
    for _, dil in DILATED_PATTERNS[::-1]:
        perm = _block_perm(dil)
        qi = perm[:, None]
        kj = np.concatenate([perm, ATTN_BLOCK + perm])[None, :]
        delta = ATTN_BLOCK + qi - kj
        band = (delta >= 0) & (delta <= ATTN_BLOCK)
        for first in (False, True):
            ok = band & (kj >= ATTN_BLOCK) if first else band
            out.append(np.tile(np.where(ok, 0.0, -np.inf).astype(np.float32), (4, 1)))
    return np.stack(out)


def _attn_kernel(cur_ref, prev_ref, bias_ref, out_ref, acc_o, acc_m, acc_l):
    first = (pl.program_id(1) == 0).astype(jnp.int32)
    lane = lax.broadcasted_iota(jnp.int32, (ATTN_BLOCK, LANES), 1)
    lo_half = lane < HEAD_DIM
    ones = jnp.ones((2 * ATTN_BLOCK, LANES), BF16)
    k_slab, v_slab = ATTN_SLABS, ATTN_SLABS + KV_PAIRS

    def rows(ref, slab, pieces):
        parts = [ref[slab, pl.ds(st, sz), :] for st, sz in pieces]
        return parts[0] if len(parts) == 1 else jnp.concatenate(parts, axis=0)

    def unit(q_pieces, prev_pieces, prev_in_cur, bias_idx, mode, out_start=None):
        bias = bias_ref[bias_idx]
        for p in range(KV_PAIRS):
            if prev_in_cur:
                kp, vp = rows(cur_ref, k_slab + p, prev_pieces), rows(cur_ref, v_slab + p, prev_pieces)
            else:
                kp, vp = rows(prev_ref, p, prev_pieces), rows(prev_ref, KV_PAIRS + p, prev_pieces)
            k2 = jnp.concatenate([kp, rows(cur_ref, k_slab + p, q_pieces)], axis=0).astype(BF16)
            v2 = jnp.concatenate([vp, rows(cur_ref, v_slab + p, q_pieces)], axis=0).astype(BF16)
            v2 = jnp.concatenate([v2, ones], axis=1)
            qs = []
            for g in range(ATTN_GROUP):
                q = rows(cur_ref, g * KV_PAIRS + p, q_pieces)
                qs += [jnp.where(lo_half, q, 0.0), jnp.where(lo_half, 0.0, q)]
            q4 = jnp.concatenate(qs, axis=0).astype(BF16)
            s = _dot_nt(q4, k2) + bias
            m = jnp.max(s, axis=-1, keepdims=True)
            pr = jnp.exp2(s - m).astype(BF16)
            ol = _dot(pr, v2)
            for g in range(ATTN_GROUP):
                r0, r1 = (2 * g) * ATTN_BLOCK, (2 * g + 1) * ATTN_BLOCK
                o = jnp.where(lo_half, ol[r0:r0 + ATTN_BLOCK, :LANES], ol[r1:r1 + ATTN_BLOCK, :LANES])
                l = jnp.where(lo_half, ol[r0:r0 + ATTN_BLOCK, LANES:], ol[r1:r1 + ATTN_BLOCK, LANES:])
                mm = jnp.where(lo_half, m[r0:r0 + ATTN_BLOCK], m[r1:r1 + ATTN_BLOCK])
                slab = g * KV_PAIRS + p
                if mode != "init":
                    m_old = rows(acc_m, slab, q_pieces)
                    m_new = jnp.maximum(m_old, mm)
                    w_old, w_new = jnp.exp2(m_old - m_new), jnp.exp2(mm - m_new)
                    o = rows(acc_o, slab, q_pieces) * w_old + o * w_new
                    l = rows(acc_l, slab, q_pieces) * w_old + l * w_new
                    mm = m_new
                if mode == "final":
                    res = o / l
                    for r in range(MAX_DIL):
                        out_ref[slab, pl.ds(out_start + r, SUBLANES, stride=MAX_DIL), :] = (
                            res[r * SUBLANES:(r + 1) * SUBLANES])
                else:
                    off = 0
                    for st, sz in q_pieces:
                        acc_o[slab, pl.ds(st, sz), :] = o[off:off + sz]
                        acc_m[slab, pl.ds(st, sz), :] = mm[off:off + sz]
                        acc_l[slab, pl.ds(st, sz), :] = l[off:off + sz]
                        off += sz

    def aligned(x, m):
        return x if isinstance(x, int) else pl.multiple_of(x, m)

    def d16_body(r, carry):
        st = pl.multiple_of(r * ATTN_BLOCK, ATTN_BLOCK)
        unit([(st, ATTN_BLOCK)], [(st, ATTN_BLOCK)], False, first, "init")
        return carry
    lax.fori_loop(0, MAX_DIL, d16_body, 0, unroll=MAX_DIL)

    def d4_pieces(c, row0):
        return [(aligned((4 * a + c) * ATTN_BLOCK + row0, 32), 32) for a in range(4)]

    def d4_body(c, carry):
        unit(d4_pieces(c, 0), d4_pieces(c, ATTN_BLOCK - 32), False, 2 + first, "merge")

        for n in range(1, 4):
            unit(d4_pieces(c, 32 * n), d4_pieces(c, 32 * (n - 1)), True, 2, "merge")
        return carry
    lax.fori_loop(0, 4, d4_body, 0, unroll=2)

    def d1_pieces(row0):
        return [(aligned(r * ATTN_BLOCK + row0, SUBLANES), SUBLANES) for r in range(MAX_DIL)]

    unit(d1_pieces(0), d1_pieces(ATTN_BLOCK - SUBLANES), False, 4 + first, "final", 0)

    def d1_body(n, carry):
        unit(d1_pieces(SUBLANES * n), d1_pieces(SUBLANES * (n - 1)), True, 4, "final", n * ATTN_BLOCK)
        return carry
    lax.fori_loop(1, SUPER // ATTN_BLOCK, d1_body, 0, unroll=5)


def _attention(qkv, bias, bsz, seq):
    t = bsz * seq
    nsb = seq // SUPER
    acc = pltpu.VMEM((ATTN_SLABS, SUPER, LANES), F32)
    return pl.pallas_call(
        _attn_kernel,
        grid=(bsz, nsb),
        in_specs=[pl.BlockSpec((QKV_SLABS, SUPER, LANES), lambda b, s: (0, b * nsb + s, 0)),
                  pl.BlockSpec((QKV_SLABS // 2, SUPER, LANES),
                               lambda b, s: (1, b * nsb + jnp.maximum(s - 1, 0), 0)),
                  _const_spec(bias.shape)],
        out_specs=pl.BlockSpec((ATTN_SLABS, SUPER, LANES), lambda b, s: (0, b * nsb + s, 0)),
        out_shape=jax.ShapeDtypeStruct((ATTN_SLABS, t, LANES), F32),
        scratch_shapes=[acc, acc, acc],
        compiler_params=pltpu.CompilerParams(dimension_semantics=("arbitrary", "arbitrary"),
                                             vmem_limit_bytes=VMEM_LIMIT),
        name="attention",
    )(qkv, qkv, bias)


def _ssd_kernel(xbc_ref, dt_ref, dtn_ref, z_ref, alog_ref, expand_ref, dskip_ref, ng_ref, out_ref,
                state_ref, y_ref, xde_ref, e_ref, *decay_refs):
    cur, nxt = decay_refs[:3], decay_refs[3:]
    c = pl.program_id(1)
    ln = SSM_CHUNK

    lane = lax.broadcasted_iota(jnp.int32, (ln, LANES), 1)
    lo_half = lane < SSM_HEAD_DIM
    part = lane // SSM_HEADS
    row = lax.broadcasted_iota(jnp.int32, (ln, ln), 0)
    col = lax.broadcasted_iota(jnp.int32, (ln, ln), 1)
    causal = row >= col
    tril = causal.astype(BF16)

    def split3(v):
        hi = v.astype(BF16).astype(F32)
        mid = (v - hi).astype(BF16).astype(F32)
        lo = (v - hi) - mid
        return hi, mid, lo

    def pack3(v):
        hi, mid, lo = split3(v)
        return jnp.where(part == 0, hi, jnp.where(part == 1, mid, jnp.where(part == 2, lo, 0.0))).astype(BF16)

    def decay_chain(dt, dst):
        a = dt * (-LOG2E * jnp.exp(alog_ref[...]))
        a3 = jnp.concatenate(split3(a), axis=0).astype(BF16)
        acum = _dot(jnp.concatenate([tril, tril, tril], axis=1), a3)
        total = acum[ln - 1:ln, :]
        packed = jnp.concatenate([pack3(dt), pack3(jnp.exp2(acum)), pack3(jnp.exp2(total - acum))], axis=0)
        packed_ref, acum_ref, acumt_ref = dst
        packed_ref[...] = packed
        acum_ref[...] = acum
        acumt_ref[...] = acum.T

    @pl.when(c == 0)
    def _():
        state_ref[...] = jnp.zeros_like(state_ref)
        decay_chain(dt_ref[...], cur)

    decay_chain(dtn_ref[...], nxt)

    bgs, cbs, y_offs = [], [], []
    for g in range(SSM_GROUPS):
        bg = xbc_ref[:, SSM_INNER + g * SSM_STATE:SSM_INNER + (g + 1) * SSM_STATE].astype(BF16)
        cg = xbc_ref[:, SSM_INNER + BC_DIM + g * SSM_STATE:SSM_INNER + BC_DIM + (g + 1) * SSM_STATE].astype(BF16)
        bgs.append(bg)
        cbs.append(_dot_nt(cg, bg))
        y_offs.append(_dot(cg, state_ref[g].astype(BF16)))

    e_ref[...] = _dot(cur[0][...], expand_ref[...])
    acum = cur[1][...]
    acum_t = cur[2][...]
    gw = SSM_INNER // SSM_GROUPS
    pairs = SSM_HEADS // SSM_GROUPS // 2
    ssq = jnp.zeros((ln, LANES), F32)
    for g in range(SSM_GROUPS):
        for j in range(pairs):
            h0 = g * 2 * pairs + 2 * j
            ps = slice(h0 * SSM_HEAD_DIM, (h0 + 2) * SSM_HEAD_DIM)
            xs = xbc_ref[:, ps]
            xdt = xs * e_ref[0:ln, ps]
            ms = []
            for h in (h0, h0 + 1):
                seg = acum[:, h:h + 1] - acum_t[h:h + 1, :]
                ms.append(cbs[g] * jnp.exp2(jnp.where(causal, seg, -jnp.inf)))
            mm = jnp.concatenate(ms, axis=1).astype(BF16)
            rhs = jnp.concatenate([jnp.where(lo_half, xdt, 0.0), jnp.where(lo_half, 0.0, xdt)],
                                  axis=0).astype(BF16)
            y = (_dot(mm, rhs) + y_offs[g][:, j * LANES:(j + 1) * LANES] * e_ref[ln:2 * ln, ps]
                 + dskip_ref[:, ps] * xs)
            y = y * _silu(z_ref[:, ps].astype(F32))
            y_ref[:, ps] = y
            ssq = ssq + y * y
            xde_ref[:, ps] = (xdt * e_ref[2 * ln:3 * ln, ps]).astype(BF16)
        gs = slice(g * gw, (g + 1) * gw)
        state_ref[g] = state_ref[g] * e_ref[2 * ln - 1:2 * ln, gs] + _dot_tn(bgs[g], xde_ref[:, gs])

    scale = lax.rsqrt(jnp.sum(ssq, axis=-1, keepdims=True) * (1.0 / SSM_INNER) + EPS)
    out_ref[...] = (y_ref[...] * scale * ng_ref[...]).astype(BF16)
    for dst, src in zip(cur, nxt):
        dst[...] = src[...]


def _ssd(xbc, dt, z, alog, expand, dskip, ng, bsz, seq):
    t = bsz * seq
    nc = seq // SSM_CHUNK
    row = lambda w: pl.BlockSpec((SSM_CHUNK, w), lambda b, c: (b * nc + c, 0))
    nxt = pl.BlockSpec((SSM_CHUNK, LANES), lambda b, c: (b * nc + jnp.minimum(c + 1, nc - 1), 0))
    decay_scratch = [pltpu.VMEM((3 * SSM_CHUNK, LANES), BF16), pltpu.VMEM((SSM_CHUNK, LANES), F32),
                     pltpu.VMEM((LANES, SSM_CHUNK), F32)]
    return pl.pallas_call(
        _ssd_kernel,
        grid=(bsz, nc),
        in_specs=[row(XBC_DIM), row(LANES), nxt, row(SSM_INNER), _const_spec(alog.shape),
                  _const_spec(expand.shape), _const_spec(dskip.shape), _const_spec(ng.shape)],
        out_specs=row(SSM_INNER),
        out_shape=jax.ShapeDtypeStruct((t, SSM_INNER), BF16),
        scratch_shapes=[pltpu.VMEM((SSM_GROUPS, SSM_STATE, SSM_INNER // SSM_GROUPS), F32),
                        pltpu.VMEM((SSM_CHUNK, SSM_INNER), F32),
                        pltpu.VMEM((SSM_CHUNK, SSM_INNER), BF16),
                        pltpu.VMEM((3 * SSM_CHUNK, SSM_INNER), F32)] + decay_scratch * 2,
        compiler_params=pltpu.CompilerParams(dimension_semantics=("arbitrary", "arbitrary"),
                                             vmem_limit_bytes=VMEM_LIMIT),
        name="ssd",
    )(xbc, dt, dt, z, alog, expand, dskip, ng)


def _tail_kernel(tiles_per_batch,
                 x_ref, attn_ref, ssm_ref, p_ref, woa_ref, wos_ref, fg_ref, wup_ref, fcw_ref, fcb_ref, wdn_ref,
                 pg_ref, wpg_ref, wpp_ref, out_ref, slabs, act_ref, carry, sbuf):
    i = pl.program_id(0)
    tm = x_ref.shape[0]
    n8 = tm // ROW_CLASSES
    nslab = D_MODEL // LANES

    @pl.when(i % tiles_per_batch == 0)
    def _():
        carry[...] = jnp.zeros_like(carry)

    attn = jnp.concatenate([attn_ref[s] for s in range(ATTN_SLABS)], axis=1).astype(BF16)
    x1 = x_ref[...] + _dot(ssm_ref[...], wos_ref[...]) + _dot(attn, woa_ref[...])

    pp = _dot(p_ref[...].astype(BF16), wpp_ref[...])

    h2 = _rms(x1) * fg_ref[...]
    for s in range(nslab):
        slabs[s] = h2[:, s * LANES:(s + 1) * LANES]
    h2p = jnp.concatenate(
        [jnp.concatenate([slabs[s, pl.ds(q, n8, stride=ROW_CLASSES), :] for q in range(ROW_CLASSES)], axis=0)
         for s in range(nslab)], axis=1).astype(BF16)

    def conv(u, cs, slot):
        blk = [u[q * n8:(q + 1) * n8] for q in range(ROW_CLASSES)]
        shifted = []
        for k in range(FFN_CONV - 1):
            sb = sbuf.at[slot * (FFN_CONV - 1) + k]
            sb[0:SUBLANES, :] = carry[k, :, cs]
            sb[SUBLANES:SUBLANES + n8, :] = blk[ROW_CLASSES - (FFN_CONV - 1) + k]
            shifted.append(sb[SUBLANES - 1:SUBLANES - 1 + n8, :])
            carry[k, :, cs] = sb[n8:n8 + SUBLANES, :]
        prev = shifted + blk
        w = [fcw_ref[k:k + 1, cs] for k in range(FFN_CONV)]
        b = fcb_ref[:, cs]
        return jnp.concatenate([b + w[2] * prev[q + 2] + w[1] * prev[q + 1] + w[0] * prev[q]
                                for q in range(ROW_CLASSES)], axis=0)

    for c in range(D_FF // FF_CHUNK):
        gs = slice(c * FF_CHUNK, (c + 1) * FF_CHUNK)
        vs = slice(D_FF + c * FF_CHUNK, D_FF + (c + 1) * FF_CHUNK)
        gate = conv(_dot(h2p, wup_ref[:, gs]), gs, 0)
        val = conv(_dot(h2p, wup_ref[:, vs]), vs, 1)
        act_ref[:, gs] = (_silu(gate) * val).astype(BF16)
    ffn_p = _dot(act_ref[...], wdn_ref[...])

    for s in range(nslab):
        for q in range(ROW_CLASSES):
            slabs[s, pl.ds(q, n8, stride=ROW_CLASSES), :] = ffn_p[q * n8:(q + 1) * n8, s * LANES:(s + 1) * LANES]
    x2 = x1 + jnp.concatenate([slabs[s] for s in range(nslab)], axis=1)

    h3 = (_rms(x2) * pg_ref[...]).astype(BF16)
    gate = jax.nn.sigmoid(_dot(h3, wpg_ref[...]))
    out_ref[...] = x2 + gate * pp


def _tail(x2d, attn, ssm, p2d, woa, wos, fg, wup, fcw, fcb, wdn, pg, wpg, wpp, seq):
    t = x2d.shape[0]
    tm = TAIL_TM
    assert seq % tm == 0 and tm % (ROW_CLASSES * SUBLANES) == 0 and D_FF % FF_CHUNK == 0
    row = lambda w: pl.BlockSpec((tm, w), lambda i: (i, 0))
    consts = [woa, wos, fg, wup, fcw, fcb, wdn, pg, wpg, wpp]
    return pl.pallas_call(
        functools.partial(_tail_kernel, seq // tm),
        grid=(t // tm,),
        in_specs=[row(D_MODEL), pl.BlockSpec((ATTN_SLABS, tm, LANES), lambda i: (0, i, 0)),
                  row(SSM_INNER), row(PLE_DIM)] + [_const_spec(w.shape) for w in consts],
        out_specs=row(D_MODEL),
        out_shape=jax.ShapeDtypeStruct((t, D_MODEL), F32),
        scratch_shapes=[pltpu.VMEM((D_MODEL // LANES, tm, LANES), F32),
                        pltpu.VMEM((tm, D_FF), BF16),
                        pltpu.VMEM((FFN_CONV - 1, SUBLANES, 2 * D_FF), F32),
                        pltpu.VMEM((2 * (FFN_CONV - 1), SUBLANES + tm // ROW_CLASSES, FF_CHUNK), F32)],
        compiler_params=pltpu.CompilerParams(dimension_semantics=("arbitrary",),
                                             vmem_limit_bytes=VMEM_LIMIT),
        name="tail",
    )(x2d, attn, ssm, p2d, *consts)


def _q_perm():
    idx = np.arange(ATTN_DIM).reshape(ATTN_KV_HEADS, ATTN_GROUP, HEAD_DIM)
    return np.transpose(idx, (1, 0, 2)).reshape(-1)


def _layer(x, p, attn_norm_g, w_in, q_norm_g, k_norm_g, ssm_conv_w, ssm_conv_b, dt_bias, a_log,
           d_skip, ssm_norm_g, w_out, ffn_norm_g, w_up, ffn_conv_w, ffn_conv_b, w_down,
           ple_norm_g, w_ple_gate, w_ple_proj):
    bsz, seq, _ = x.shape
    t = bsz * seq
    x2d = x.reshape(t, D_MODEL)
    p2d = p.reshape(t, PLE_DIM)
    qperm = _q_perm()

    o_k, o_v, o_z = ATTN_DIM, ATTN_DIM + KV_DIM, ATTN_DIM + 2 * KV_DIM
    o_xbc, o_dt = o_z + SSM_INNER, o_z + SSM_INNER + XBC_DIM
    wqk = jnp.concatenate([w_in[:, :ATTN_DIM][:, qperm], w_in[:, o_k:o_v]], axis=1).astype(BF16)
    wv = w_in[:, o_v:o_z].astype(BF16)
    wz = w_in[:, o_z:o_xbc].astype(BF16)
    wxbc = w_in[:, o_xbc:o_dt].astype(BF16)
    head_reps = LANES // SSM_HEADS
    wdt = jnp.tile(w_in[:, o_dt:], (1, head_reps)).astype(BF16)
    qkg = jnp.concatenate([jnp.tile(q_norm_g * (HEAD_DIM ** -0.5 * LOG2E), ATTN_Q_HEADS),
                           jnp.tile(k_norm_g, ATTN_KV_HEADS)]).reshape(1, QK_DIM)
    hid = np.arange(256) // HEAD_DIM
    bd = jnp.asarray(hid[:, None] == hid[None, :], BF16)
    dtb = jnp.tile(dt_bias, head_reps).reshape(1, LANES)
    alog = jnp.tile(a_log, head_reps).reshape(1, LANES)

    qkv, z, xbc, dt = _in_proj(x2d, attn_norm_g.reshape(1, D_MODEL), wqk, wv, wz, wxbc, wdt, qkg, bd,
                               ssm_conv_w, ssm_conv_b.reshape(1, XBC_DIM), dtb, seq)

    attn = _attention(qkv, jnp.asarray(_attn_bias()), bsz, seq)

    src_lane = np.arange(LANES)[:, None]
    expand = jnp.asarray((src_lane % SSM_HEADS == (np.arange(SSM_INNER) // SSM_HEAD_DIM)[None, :])
                         & (src_lane < 3 * SSM_HEADS), BF16)
    dskip = jnp.repeat(d_skip, SSM_HEAD_DIM).reshape(1, SSM_INNER)
    ssm = _ssd(xbc, dt, z, alog, expand, dskip, ssm_norm_g.reshape(1, SSM_INNER), bsz, seq)

    woa = w_out[:ATTN_DIM][qperm].astype(BF16)
    wos = w_out[ATTN_DIM:].astype(BF16)
    out = _tail(x2d, attn, ssm, p2d, woa, wos, ffn_norm_g.reshape(1, D_MODEL), w_up.astype(BF16),
                ffn_conv_w, ffn_conv_b.reshape(1, 2 * D_FF), w_down.astype(BF16),
                ple_norm_g.reshape(1, D_MODEL), w_ple_gate.astype(BF16), w_ple_proj.astype(BF16), seq)
    return out.reshape(bsz, seq, D_MODEL)


def kernel(x, p, attn_norm_g, w_in, q_norm_g, k_norm_g, ssm_conv_w, ssm_conv_b, dt_bias, a_log, d_skip,
           ssm_norm_g, w_out, ffn_norm_g, w_up, ffn_conv_w, ffn_conv_b, w_down, ple_norm_g, w_ple_gate,
           w_ple_proj):
    depth = w_in.shape[0]
    for i in range(depth):
        x = _layer(x, p[i], attn_norm_g[i], w_in[i], q_norm_g[i], k_norm_g[i], ssm_conv_w[i], ssm_conv_b[i],
                   dt_bias[i], a_log[i], d_skip[i], ssm_norm_g[i], w_out[i], ffn_norm_g[i], w_up[i],
                   ffn_conv_w[i], ffn_conv_b[i], w_down[i], ple_norm_g[i], w_ple_gate[i], w_ple_proj[i])
    return x
```

```python
import functools

import jax
import jax.numpy as jnp
import numpy as np
from jax import lax
from jax.experimental import pallas as pl
from jax.experimental.pallas import tpu as pltpu

F32 = jnp.float32
BF16 = jnp.bfloat16

D_MODEL = 1024
HEAD_DIM = 64
ATTN_Q_HEADS = 8
ATTN_KV_HEADS = 4
ATTN_GROUP = ATTN_Q_HEADS // ATTN_KV_HEADS
DILATED_PATTERNS = ((128, 1), (512, 4), (2048, 16))
ATTN_BLOCK = 128
ATTN_DIM = ATTN_Q_HEADS * HEAD_DIM
KV_DIM = ATTN_KV_HEADS * HEAD_DIM
SSM_HEADS = 16
SSM_HEAD_DIM = 64
SSM_INNER = SSM_HEADS * SSM_HEAD_DIM
SSM_GROUPS = 2
SSM_STATE = 128
SSM_CONV = 4
SSM_CHUNK = 128
BC_DIM = SSM_GROUPS * SSM_STATE
XBC_DIM = SSM_INNER + 2 * BC_DIM
D_FF = 2816
FFN_CONV = 3
PLE_DIM = 256
EPS = 1e-6
LOG2E = 1.4426950408889634

LANES = 128
SUBLANES = 8
QK_DIM = ATTN_DIM + KV_DIM
QKV_DIM = QK_DIM + KV_DIM
QKV_SLABS = QKV_DIM // LANES
ATTN_SLABS = ATTN_DIM // LANES
KV_PAIRS = ATTN_KV_HEADS // 2
VMEM_LIMIT = 56 * 1024 * 1024

MAX_DIL = 16
SUPER = MAX_DIL * ATTN_BLOCK
IN_TM = 512
XBC_CHUNK = 512
SSD_STEP_CHUNKS = 4
TAIL_TM = 512
ROW_CLASSES = 8
FF_CHUNK = 256

assert DILATED_PATTERNS == ((128, 1), (512, 4), (2048, 16))
assert all(w // d == ATTN_BLOCK for w, d in DILATED_PATTERNS)


def _dot(a, b):
    return jnp.dot(a, b, preferred_element_type=F32)


def _dot_nt(a, b):
    return lax.dot_general(a, b, (((1,), (1,)), ((), ())), preferred_element_type=F32)


def _dot_tn(a, b):
    return lax.dot_general(a, b, (((0,), (0,)), ((), ())), preferred_element_type=F32)


def _rms(x):
    return x * lax.rsqrt(jnp.mean(x * x, axis=-1, keepdims=True) + EPS)


def _silu(x):
    return x * jax.nn.sigmoid(x)


def _const_spec(shape):
    nd = len(shape)
    return pl.BlockSpec(shape, lambda *_: (0,) * nd, pipeline_mode=pl.Buffered(1))


def _in_proj_kernel(tiles_per_batch,
                    x_ref, g_ref, wqk_ref, wv_ref, wz_ref, wxbc_ref, wdt_ref, qkg_ref, bd_ref,
                    cw_ref, cb_ref, dtb_ref,
                    qkv_ref, z_ref, xbc_ref, dt_ref, nat, mod4, *cbufs):
    i = pl.program_id(0)
    tm = x_ref.shape[0]

    @pl.when(i % tiles_per_batch == 0)
    def _():
        for cbuf in cbufs:
            cbuf[0:SUBLANES, :] = jnp.zeros((SUBLANES, XBC_CHUNK), F32)

    hb = (_rms(x_ref[...]) * g_ref[...]).astype(BF16)

    def ssm_conv(c):
        cbuf = cbufs[c]
        cs = slice(c * XBC_CHUNK, (c + 1) * XBC_CHUNK)
        cbuf[SUBLANES:SUBLANES + tm, :] = _dot(hb, wxbc_ref[:, cs])
        acc = cb_ref[:, cs] + cw_ref[SSM_CONV - 1:SSM_CONV, cs] * cbuf[SUBLANES:SUBLANES + tm, :]
        for kk in range(SSM_CONV - 1):
            off = SUBLANES - (SSM_CONV - 1) + kk
            acc = acc + cw_ref[kk:kk + 1, cs] * cbuf[off:off + tm, :]
        xbc_ref[:, cs] = _silu(acc)
        cbuf[0:SUBLANES, :] = cbuf[tm:tm + SUBLANES, :]

    def qk_norm():
        qk = _dot(hb, wqk_ref[...])
        sq = qk * qk
        sq_hi = sq.astype(BF16)
        sq_lo = (sq - sq_hi.astype(F32)).astype(BF16)
        bd = bd_ref[...]
        for c in range(QK_DIM // 256):
            sl = slice(c * 256, (c + 1) * 256)
            ss = _dot(sq_hi[:, sl], bd) + _dot(sq_lo[:, sl], bd)
            qkn = qk[:, sl] * lax.rsqrt(ss * (1.0 / HEAD_DIM) + EPS) * qkg_ref[:, sl]
            nat[2 * c] = qkn[:, :LANES]
            nat[2 * c + 1] = qkn[:, LANES:]

    def v_proj():
        v = _dot(hb, wv_ref[...])
        nat[QKV_SLABS - 2] = v[:, :LANES]
        nat[QKV_SLABS - 1] = v[:, LANES:]

    def relayout():
        quarter = tm // 4
        sixteenth = tm // MAX_DIL
        for s in range(QKV_SLABS):
            for a in range(4):
                mod4[s, a * quarter:(a + 1) * quarter, :] = nat[s, pl.ds(a, quarter, stride=4), :]
        for s in range(QKV_SLABS):
            for a in range(4):
                for b in range(4):
                    qkv_ref[s, 0, 4 * b + a, 0] = mod4[s, pl.ds(a * quarter + b, sixteenth, stride=4), :]

    def z_proj():
        z_ref[...] = _dot(hb, wz_ref[...]).astype(BF16)

    def dt_proj():
        dtr = _dot(hb, wdt_ref[...]) + dtb_ref[...]
        dt_ref[...] = jnp.maximum(dtr, 0.0) + jnp.log1p(jnp.exp(-jnp.abs(dtr)))

    for c in range(len(cbufs)):
        ssm_conv(c)
    qk_norm()
    v_proj()
    relayout()
    z_proj()
    dt_proj()


def _in_proj(x2d, g, wqk, wv, wz, wxbc, wdt, qkg, bd, cw, cb, dtb, seq):
    t = x2d.shape[0]
    tm = IN_TM
    assert SUPER % tm == 0 and tm % (4 * MAX_DIL) == 0 and seq % SUPER == 0
    grid = (t // tm,)
    row = lambda w: pl.BlockSpec((tm, w), lambda i: (i, 0))
    per_super = SUPER // tm
    qkv_shape = (QKV_SLABS, t // SUPER, MAX_DIL, per_super, tm // MAX_DIL, LANES)
    qkv, z, xbc, dt = pl.pallas_call(
        functools.partial(_in_proj_kernel, seq // tm),
        grid=grid,
        in_specs=[row(D_MODEL), _const_spec(g.shape), _const_spec(wqk.shape), _const_spec(wv.shape),
                  _const_spec(wz.shape), _const_spec(wxbc.shape), _const_spec(wdt.shape),
                  _const_spec(qkg.shape), _const_spec(bd.shape), _const_spec(cw.shape),
                  _const_spec(cb.shape), _const_spec(dtb.shape)],
        out_specs=[pl.BlockSpec((QKV_SLABS, 1, MAX_DIL, 1, tm // MAX_DIL, LANES),
                                lambda i: (0, i // per_super, 0, i % per_super, 0, 0)),
                   row(SSM_INNER), row(XBC_DIM), row(LANES)],
        out_shape=[jax.ShapeDtypeStruct(qkv_shape, F32), jax.ShapeDtypeStruct((t, SSM_INNER), BF16),
                   jax.ShapeDtypeStruct((t, XBC_DIM), F32), jax.ShapeDtypeStruct((t, LANES), F32)],
        scratch_shapes=[pltpu.VMEM((QKV_SLABS, tm, LANES), F32),
                        pltpu.VMEM((QKV_SLABS, tm, LANES), F32)]
                       + [pltpu.VMEM((tm + SUBLANES, XBC_CHUNK), F32)] * (XBC_DIM // XBC_CHUNK),
        compiler_params=pltpu.CompilerParams(dimension_semantics=("arbitrary",),
                                             vmem_limit_bytes=VMEM_LIMIT),
        name="in_proj",
    )(x2d, g, wqk, wv, wz, wxbc, wdt, qkg, bd, cw, cb, dtb)
    return qkv.reshape(QKV_SLABS, t, LANES), z, xbc, dt


def _block_perm(dil):
    reps = MAX_DIL // dil
    per = ATTN_BLOCK // reps
    pos = np.arange(ATTN_BLOCK)
    return (pos % per) * reps + pos // per


def _attn_bias():
    out = []
    for _, dil in DILATED_PATTERNS[::-1]:
        perm = _block_perm(dil)
        qi = perm[:, None]
        kj = np.concatenate([perm, ATTN_BLOCK + perm])[None, :]
        delta = ATTN_BLOCK + qi - kj
        band = (delta >= 0) & (delta <= ATTN_BLOCK)
        for first in (False, True):
            ok = band & (kj >= ATTN_BLOCK) if first else band
            out.append(np.tile(np.where(ok, 0.0, -np.inf).astype(np.float32), (4, 1)))
    return np.stack(out)


def _attn_kernel(cur_ref, prev_ref, bias_ref, out_ref, acc_o, acc_m, acc_l):
    first = (pl.program_id(1) == 0).astype(jnp.int32)
    lane = lax.broadcasted_iota(jnp.int32, (ATTN_BLOCK, LANES), 1)
    lo_half = lane < HEAD_DIM
    ones = jnp.ones((2 * ATTN_BLOCK, LANES), BF16)
    k_slab, v_slab = ATTN_SLABS, ATTN_SLABS + KV_PAIRS

    def rows(ref, slab, pieces):
        parts = [ref[slab, pl.ds(st, sz), :] for st, sz in pieces]
        return parts[0] if len(parts) == 1 else jnp.concatenate(parts, axis=0)

    def unit(q_pieces, prev_pieces, prev_in_cur, bias_idx, mode, out_start=None):
        bias = bias_ref[bias_idx]
        for p in range(KV_PAIRS):
            if prev_in_cur:
                kp, vp = rows(cur_ref, k_slab + p, prev_pieces), rows(cur_ref, v_slab + p, prev_pieces)
            else:
                kp, vp = rows(prev_ref, p, prev_pieces), rows(prev_ref, KV_PAIRS + p, prev_pieces)
            k2 = jnp.concatenate([kp, rows(cur_ref, k_slab + p, q_pieces)], axis=0).astype(BF16)
            v2 = jnp.concatenate([vp, rows(cur_ref, v_slab + p, q_pieces)], axis=0).astype(BF16)
            v2 = jnp.concatenate([v2, ones], axis=1)
            qs = []
            for g in range(ATTN_GROUP):
                q = rows(cur_ref, g * KV_PAIRS + p, q_pieces)
                qs += [jnp.where(lo_half, q, 0.0), jnp.where(lo_half, 0.0, q)]
            q4 = jnp.concatenate(qs, axis=0).astype(BF16)
            s = _dot_nt(q4, k2) + bias
            m = jnp.max(s, axis=-1, keepdims=True)
            pr = jnp.exp2(s - m).astype(BF16)
            ol = _dot(pr, v2)
            for g in range(ATTN_GROUP):
                r0, r1 = (2 * g) * ATTN_BLOCK, (2 * g + 1) * ATTN_BLOCK
                o = jnp.where(lo_half, ol[r0:r0 + ATTN_BLOCK, :LANES], ol[r1:r1 + ATTN_BLOCK, :LANES])
                l = jnp.where(lo_half, ol[r0:r0 + ATTN_BLOCK, LANES:], ol[r1:r1 + ATTN_BLOCK, LANES:])
                mm = jnp.where(lo_half, m[r0:r0 + ATTN_BLOCK], m[r1:r1 + ATTN_BLOCK])
                slab = g * KV_PAIRS + p
                if mode != "init":
                    m_old = rows(acc_m, slab, q_pieces)
                    m_new = jnp.maximum(m_old, mm)
                    w_old, w_new = jnp.exp2(m_old - m_new), jnp.exp2(mm - m_new)
                    o = rows(acc_o, slab, q_pieces) * w_old + o * w_new
                    l = rows(acc_l, slab, q_pieces) * w_old + l * w_new
                    mm = m_new
                if mode == "final":
                    res = o / l
                    for r in range(MAX_DIL):
                        out_ref[slab, pl.ds(out_start + r, SUBLANES, stride=MAX_DIL), :] = (
                            res[r * SUBLANES:(r + 1) * SUBLANES])
                else:
                    off = 0
                    for st, sz in q_pieces:
                        acc_o[slab, pl.ds(st, sz), :] = o[off:off + sz]
                        acc_m[slab, pl.ds(st, sz), :] = mm[off:off + sz]
                        acc_l[slab, pl.ds(st, sz), :] = l[off:off + sz]
                        off += sz

    def aligned(x, m):
        return x if isinstance(x, int) else pl.multiple_of(x, m)

    def d16_body(r, carry):
        st = pl.multiple_of(r * ATTN_BLOCK, ATTN_BLOCK)
        unit([(st, ATTN_BLOCK)], [(st, ATTN_BLOCK)], False, first, "init")
        return carry
    lax.fori_loop(0, MAX_DIL, d16_body, 0, unroll=MAX_DIL)

    def d4_pieces(c, row0):
        return [(aligned((4 * a + c) * ATTN_BLOCK + row0, 32), 32) for a in range(4)]

    def d4_body(c, carry):
        unit(d4_pieces(c, 0), d4_pieces(c, ATTN_BLOCK - 32), False, 2 + first, "merge")

        for n in range(1, 4):
            unit(d4_pieces(c, 32 * n), d4_pieces(c, 32 * (n - 1)), True, 2, "merge")
        return carry
    lax.fori_loop(0, 4, d4_body, 0, unroll=2)

    def d1_pieces(row0):
        return [(aligned(r * ATTN_BLOCK + row0, SUBLANES), SUBLANES) for r in range(MAX_DIL)]

    unit(d1_pieces(0), d1_pieces(ATTN_BLOCK - SUBLANES), False, 4 + first, "final", 0)

    def d1_body(n, carry):
        unit(d1_pieces(SUBLANES * n), d1_pieces(SUBLANES * (n - 1)), True, 4, "final", n * ATTN_BLOCK)
        return carry
    lax.fori_loop(1, SUPER // ATTN_BLOCK, d1_body, 0, unroll=5)


def _attention(qkv, bias, bsz, seq):
    t = bsz * seq
    nsb = seq // SUPER
    acc = pltpu.VMEM((ATTN_SLABS, SUPER, LANES), F32)
    return pl.pallas_call(
        _attn_kernel,
        grid=(bsz, nsb),
        in_specs=[pl.BlockSpec((QKV_SLABS, SUPER, LANES), lambda b, s: (0, b * nsb + s, 0)),
                  pl.BlockSpec((QKV_SLABS // 2, SUPER, LANES),
                               lambda b, s: (1, b * nsb + jnp.maximum(s - 1, 0), 0)),
                  _const_spec(bias.shape)],
        out_specs=pl.BlockSpec((ATTN_SLABS, SUPER, LANES), lambda b, s: (0, b * nsb + s, 0)),
        out_shape=jax.ShapeDtypeStruct((ATTN_SLABS, t, LANES), F32),
        scratch_shapes=[acc, acc, acc],
        compiler_params=pltpu.CompilerParams(dimension_semantics=("arbitrary", "arbitrary"),
                                             vmem_limit_bytes=VMEM_LIMIT),
        name="attention",
    )(qkv, qkv, bias)


def _ssd_kernel(xbc_ref, dt_ref, dtn_ref, z_ref, alog_ref, expand_ref, dskip_ref, ng_ref, out_ref,
                state_ref, y_ref, xde_ref, e_ref, *decay_refs):
    nsub = SSD_STEP_CHUNKS
    cur = [decay_refs[3 * k:3 * k + 3] for k in range(nsub)]
    nxt = [decay_refs[3 * (nsub + k):3 * (nsub + k) + 3] for k in range(nsub)]
    c = pl.program_id(1)
    ln = SSM_CHUNK

    lane = lax.broadcasted_iota(jnp.int32, (ln, LANES), 1)
    lo_half = lane < SSM_HEAD_DIM
    part = lane // SSM_HEADS
    row = lax.broadcasted_iota(jnp.int32, (ln, ln), 0)
    col = lax.broadcasted_iota(jnp.int32, (ln, ln), 1)
    causal = row >= col
    tril = causal.astype(BF16)

    def split3(v):
        hi = v.astype(BF16).astype(F32)
        mid = (v - hi).astype(BF16).astype(F32)
        lo = (v - hi) - mid
        return hi, mid, lo

    def pack3(v):
        hi, mid, lo = split3(v)
        return jnp.where(part == 0, hi, jnp.where(part == 1, mid, jnp.where(part == 2, lo, 0.0))).astype(BF16)

    def decay_chain(dt, dst):
        a = dt * (-LOG2E * jnp.exp(alog_ref[...]))
        a3 = jnp.concatenate(split3(a), axis=0).astype(BF16)
        acum = _dot(jnp.concatenate([tril, tril, tril], axis=1), a3)
        total = acum[ln - 1:ln, :]
        packed = jnp.concatenate([pack3(dt), pack3(jnp.exp2(acum)), pack3(jnp.exp2(total - acum))], axis=0)
        packed_ref, acum_ref, acumt_ref = dst
        packed_ref[...] = packed
        acum_ref[...] = acum
        acumt_ref[...] = acum.T

    @pl.when(c == 0)
    def _():
        state_ref[...] = jnp.zeros_like(state_ref)
        for k in range(nsub):
            decay_chain(dt_ref[k * ln:(k + 1) * ln, :], cur[k])

    for k in range(nsub):
        decay_chain(dtn_ref[k * ln:(k + 1) * ln, :], nxt[k])

    gw = SSM_INNER // SSM_GROUPS
    pairs = SSM_HEADS // SSM_GROUPS // 2
    for k in range(nsub):
        rows = slice(k * ln, (k + 1) * ln)
        e = e_ref.at[k]
        xde = xde_ref.at[k]
        bgs, cbs, y_offs = [], [], []
        for g in range(SSM_GROUPS):
            bg = xbc_ref[rows, SSM_INNER + g * SSM_STATE:SSM_INNER + (g + 1) * SSM_STATE].astype(BF16)
            cg = xbc_ref[rows, SSM_INNER + BC_DIM + g * SSM_STATE:
                         SSM_INNER + BC_DIM + (g + 1) * SSM_STATE].astype(BF16)
            bgs.append(bg)
            cbs.append(_dot_nt(cg, bg))
            y_offs.append(_dot(cg, state_ref[g].astype(BF16)))

        e[...] = _dot(cur[k][0][...], expand_ref[...])
        acum = cur[k][1][...]
        acum_t = cur[k][2][...]
        ssq = jnp.zeros((ln, LANES), F32)
        for g in range(SSM_GROUPS):
            for j in range(pairs):
                h0 = g * 2 * pairs + 2 * j
                ps = slice(h0 * SSM_HEAD_DIM, (h0 + 2) * SSM_HEAD_DIM)
                xs = xbc_ref[rows, ps]
                xdt = xs * e[0:ln, ps]
                ms = []
                for h in (h0, h0 + 1):
                    seg = acum[:, h:h + 1] - acum_t[h:h + 1, :]
                    ms.append(cbs[g] * jnp.exp2(jnp.where(causal, seg, -jnp.inf)))
                mm = jnp.concatenate(ms, axis=1).astype(BF16)
                rhs = jnp.concatenate([jnp.where(lo_half, xdt, 0.0), jnp.where(lo_half, 0.0, xdt)],
                                      axis=0).astype(BF16)
                y = (_dot(mm, rhs) + y_offs[g][:, j * LANES:(j + 1) * LANES] * e[ln:2 * ln, ps]
                     + dskip_ref[:, ps] * xs)
                y = y * _silu(z_ref[rows, ps].astype(F32))
                y_ref[rows, ps] = y
                ssq = ssq + y * y
                xde[:, ps] = (xdt * e[2 * ln:3 * ln, ps]).astype(BF16)
            gs = slice(g * gw, (g + 1) * gw)
            state_ref[g] = state_ref[g] * e[2 * ln - 1:2 * ln, gs] + _dot_tn(bgs[g], xde[:, gs])

        scale = lax.rsqrt(jnp.sum(ssq, axis=-1, keepdims=True) * (1.0 / SSM_INNER) + EPS)
        out_ref[rows, :] = (y_ref[rows, :] * scale * ng_ref[...]).astype(BF16)

    for k in range(nsub):
        for dst, src in zip(cur[k], nxt[k]):
            dst[...] = src[...]


def _ssd(xbc, dt, z, alog, expand, dskip, ng, bsz, seq):
    t = bsz * seq
    nsub = SSD_STEP_CHUNKS
    rows = nsub * SSM_CHUNK
    assert seq % rows == 0
    ns = seq // rows
    row = lambda w: pl.BlockSpec((rows, w), lambda b, c: (b * ns + c, 0))
    nxt = pl.BlockSpec((rows, LANES), lambda b, c: (b * ns + jnp.minimum(c + 1, ns - 1), 0))
    decay_scratch = [pltpu.VMEM((3 * SSM_CHUNK, LANES), BF16), pltpu.VMEM((SSM_CHUNK, LANES), F32),
                     pltpu.VMEM((LANES, SSM_CHUNK), F32)]
    return pl.pallas_call(
        _ssd_kernel,
        grid=(bsz, ns),
        in_specs=[row(XBC_DIM), row(LANES), nxt, row(SSM_INNER), _const_spec(alog.shape),
                  _const_spec(expand.shape), _const_spec(dskip.shape), _const_spec(ng.shape)],
        out_specs=row(SSM_INNER),
        out_shape=jax.ShapeDtypeStruct((t, SSM_INNER), BF16),
        scratch_shapes=[pltpu.VMEM((SSM_GROUPS, SSM_STATE, SSM_INNER // SSM_GROUPS), F32),
                        pltpu.VMEM((rows, SSM_INNER), F32),
                        pltpu.VMEM((nsub, SSM_CHUNK, SSM_INNER), BF16),
                        pltpu.VMEM((nsub, 3 * SSM_CHUNK, SSM_INNER), F32)] + decay_scratch * (2 * nsub),
        compiler_params=pltpu.CompilerParams(dimension_semantics=("arbitrary", "arbitrary"),
                                             vmem_limit_bytes=VMEM_LIMIT),
        name="ssd",
    )(xbc, dt, dt, z, alog, expand, dskip, ng)


def _tail_kernel(tiles_per_batch,
                 x_ref, attn_ref, ssm_ref, p_ref, woa_ref, wos_ref, fg_ref, wup_ref, fcw_ref, fcb_ref, wdn_ref,
                 pg_ref, wpg_ref, wpp_ref, out_ref, slabs, act_ref, carry, sbuf):
    i = pl.program_id(0)
    tm = x_ref.shape[0]
    n8 = tm // ROW_CLASSES
    nslab = D_MODEL // LANES

    @pl.when(i % tiles_per_batch == 0)
    def _():
        carry[...] = jnp.zeros_like(carry)

    attn = jnp.concatenate([attn_ref[s] for s in range(ATTN_SLABS)], axis=1).astype(BF16)
    x1 = x_ref[...] + _dot(ssm_ref[...], wos_ref[...]) + _dot(attn, woa_ref[...])

    pp = _dot(p_ref[...].astype(BF16), wpp_ref[...])

    h2 = _rms(x1) * fg_ref[...]
    for s in range(nslab):
        slabs[s] = h2[:, s * LANES:(s + 1) * LANES]
    h2p = jnp.concatenate(
        [jnp.concatenate([slabs[s, pl.ds(q, n8, stride=ROW_CLASSES), :] for q in range(ROW_CLASSES)], axis=0)
         for s in range(nslab)], axis=1).astype(BF16)

    def conv(u, cs, slot):
        blk = [u[q * n8:(q + 1) * n8] for q in range(ROW_CLASSES)]
        shifted = []
        for k in range(FFN_CONV - 1):
            sb = sbuf.at[slot * (FFN_CONV - 1) + k]
            sb[0:SUBLANES, :] = carry[k, :, cs]
            sb[SUBLANES:SUBLANES + n8, :] = blk[ROW_CLASSES - (FFN_CONV - 1) + k]
            shifted.append(sb[SUBLANES - 1:SUBLANES - 1 + n8, :])
            carry[k, :, cs] = sb[n8:n8 + SUBLANES, :]
        prev = shifted + blk
        w = [fcw_ref[k:k + 1, cs] for k in range(FFN_CONV)]
        b = fcb_ref[:, cs]
        return jnp.concatenate([b + w[2] * prev[q + 2] + w[1] * prev[q + 1] + w[0] * prev[q]
                                for q in range(ROW_CLASSES)], axis=0)

    for c in range(D_FF // FF_CHUNK):
        gs = slice(c * FF_CHUNK, (c + 1) * FF_CHUNK)
        vs = slice(D_FF + c * FF_CHUNK, D_FF + (c + 1) * FF_CHUNK)
        gate = conv(_dot(h2p, wup_ref[:, gs]), gs, 0)
        val = conv(_dot(h2p, wup_ref[:, vs]), vs, 1)
        act_ref[:, gs] = (_silu(gate) * val).astype(BF16)
    ffn_p = _dot(act_ref[...], wdn_ref[...])

    for s in range(nslab):
        for q in range(ROW_CLASSES):
            slabs[s, pl.ds(q, n8, stride=ROW_CLASSES), :] = ffn_p[q * n8:(q + 1) * n8, s * LANES:(s + 1) * LANES]
    x2 = x1 + jnp.concatenate([slabs[s] for s in range(nslab)], axis=1)

    h3 = (_rms(x2) * pg_ref[...]).astype(BF16)
    gate = jax.nn.sigmoid(_dot(h3, wpg_ref[...]))
    out_ref[...] = x2 + gate * pp


def _tail(x2d, attn, ssm, p2d, woa, wos, fg, wup, fcw, fcb, wdn, pg, wpg, wpp, seq):
    t = x2d.shape[0]
    tm = TAIL_TM
    assert seq % tm == 0 and tm % (ROW_CLASSES * SUBLANES) == 0 and D_FF % FF_CHUNK == 0
    row = lambda w: pl.BlockSpec((tm, w), lambda i: (i, 0))
    consts = [woa, wos, fg, wup, fcw, fcb, wdn, pg, wpg, wpp]
    return pl.pallas_call(
        functools.partial(_tail_kernel, seq // tm),
        grid=(t // tm,),
        in_specs=[row(D_MODEL), pl.BlockSpec((ATTN_SLABS, tm, LANES), lambda i: (0, i, 0)),
                  row(SSM_INNER), row(PLE_DIM)] + [_const_spec(w.shape) for w in consts],
        out_specs=row(D_MODEL),
        out_shape=jax.ShapeDtypeStruct((t, D_MODEL), F32),
        scratch_shapes=[pltpu.VMEM((D_MODEL // LANES, tm, LANES), F32),
                        pltpu.VMEM((tm, D_FF), BF16),
                        pltpu.VMEM((FFN_CONV - 1, SUBLANES, 2 * D_FF), F32),
                        pltpu.VMEM((2 * (FFN_CONV - 1), SUBLANES + tm // ROW_CLASSES, FF_CHUNK), F32)],
        compiler_params=pltpu.CompilerParams(dimension_semantics=("arbitrary",),
                                             vmem_limit_bytes=VMEM_LIMIT),
        name="tail",
    )(x2d, attn, ssm, p2d, *consts)


def _q_perm():
    idx = np.arange(ATTN_DIM).reshape(ATTN_KV_HEADS, ATTN_GROUP, HEAD_DIM)
    return np.transpose(idx, (1, 0, 2)).reshape(-1)


def _layer(x, p, attn_norm_g, w_in, q_norm_g, k_norm_g, ssm_conv_w, ssm_conv_b, dt_bias, a_log,
           d_skip, ssm_norm_g, w_out, ffn_norm_g, w_up, ffn_conv_w, ffn_conv_b, w_down,
           ple_norm_g, w_ple_gate, w_ple_proj):
    bsz, seq, _ = x.shape
    t = bsz * seq
    x2d = x.reshape(t, D_MODEL)
    p2d = p.reshape(t, PLE_DIM)
    qperm = _q_perm()

    o_k, o_v, o_z = ATTN_DIM, ATTN_DIM + KV_DIM, ATTN_DIM + 2 * KV_DIM
    o_xbc, o_dt = o_z + SSM_INNER, o_z + SSM_INNER + XBC_DIM
    wqk = jnp.concatenate([w_in[:, :ATTN_DIM][:, qperm], w_in[:, o_k:o_v]], axis=1).astype(BF16)
    wv = w_in[:, o_v:o_z].astype(BF16)
    wz = w_in[:, o_z:o_xbc].astype(BF16)
    wxbc = w_in[:, o_xbc:o_dt].astype(BF16)
    head_reps = LANES // SSM_HEADS
    wdt = jnp.tile(w_in[:, o_dt:], (1, head_reps)).astype(BF16)
    qkg = jnp.concatenate([jnp.tile(q_norm_g * (HEAD_DIM ** -0.5 * LOG2E), ATTN_Q_HEADS),
                           jnp.tile(k_norm_g, ATTN_KV_HEADS)]).reshape(1, QK_DIM)
    hid = np.arange(256) // HEAD_DIM
    bd = jnp.asarray(hid[:, None] == hid[None, :], BF16)
    dtb = jnp.tile(dt_bias, head_reps).reshape(1, LANES)
    alog = jnp.tile(a_log, head_reps).reshape(1, LANES)

    qkv, z, xbc, dt = _in_proj(x2d, attn_norm_g.reshape(1, D_MODEL), wqk, wv, wz, wxbc, wdt, qkg, bd,
                               ssm_conv_w, ssm_conv_b.reshape(1, XBC_DIM), dtb, seq)

    attn = _attention(qkv, jnp.asarray(_attn_bias()), bsz, seq)

    src_lane = np.arange(LANES)[:, None]
    expand = jnp.asarray((src_lane % SSM_HEADS == (np.arange(SSM_INNER) // SSM_HEAD_DIM)[None, :])
                         & (src_lane < 3 * SSM_HEADS), BF16)
    dskip = jnp.repeat(d_skip, SSM_HEAD_DIM).reshape(1, SSM_INNER)
    ssm = _ssd(xbc, dt, z, alog, expand, dskip, ssm_norm_g.reshape(1, SSM_INNER), bsz, seq)

    woa = w_out[:ATTN_DIM][qperm].astype(BF16)
    wos = w_out[ATTN_DIM:].astype(BF16)
    out = _tail(x2d, attn, ssm, p2d, woa, wos, ffn_norm_g.reshape(1, D_MODEL), w_up.astype(BF16),
                ffn_conv_w, ffn_conv_b.reshape(1, 2 * D_FF), w_down.astype(BF16),
                ple_norm_g.reshape(1, D_MODEL), w_ple_gate.astype(BF16), w_ple_proj.astype(BF16), seq)
    return out.reshape(bsz, seq, D_MODEL)


def kernel(x, p, attn_norm_g, w_in, q_norm_g, k_norm_g, ssm_conv_w, ssm_conv_b, dt_bias, a_log, d_skip,
           ssm_norm_g, w_out, ffn_norm_g, w_up, ffn_conv_w, ffn_conv_b, w_down, ple_norm_g, w_ple_gate,
           w_ple_proj):
    depth = w_in.shape[0]
    for i in range(depth):
        x = _layer(x, p[i], attn_norm_g[i], w_in[i], q_norm_g[i], k_norm_g[i], ssm_conv_w[i], ssm_conv_b[i],
                   dt_bias[i], a_log[i], d_skip[i], ssm_norm_g[i], w_out[i], ffn_norm_g[i], w_up[i],
                   ffn_conv_w[i], ffn_conv_b[i], w_down[i], ple_norm_g[i], w_ple_gate[i], w_ple_proj[i])
    return x
```

```python
import functools

import jax
import jax.numpy as jnp
import numpy as np
from jax import lax
from jax.experimental import pallas as pl
from jax.experimental.pallas import tpu as pltpu

F32 = jnp.float32
BF16 = jnp.bfloat16

D_MODEL = 1024
HEAD_DIM = 64
ATTN_Q_HEADS = 8
ATTN_KV_HEADS = 4
ATTN_GROUP = ATTN_Q_HEADS // ATTN_KV_HEADS
DILATED_PATTERNS = ((128, 1), (512, 4), (2048, 16))
ATTN_BLOCK = 128
ATTN_DIM = ATTN_Q_HEADS * HEAD_DIM
KV_DIM = ATTN_KV_HEADS * HEAD_DIM
SSM_HEADS = 16
SSM_HEAD_DIM = 64
SSM_INNER = SSM_HEADS * SSM_HEAD_DIM
SSM_GROUPS = 2
SSM_STATE = 128
SSM_CONV = 4
SSM_CHUNK = 128
BC_DIM = SSM_GROUPS * SSM_STATE
XBC_DIM = SSM_INNER + 2 * BC_DIM
D_FF = 2816
FFN_CONV = 3
PLE_DIM = 256
EPS = 1e-6
LOG2E = 1.4426950408889634

LANES = 128
SUBLANES = 8
QK_DIM = ATTN_DIM + KV_DIM
QKV_DIM = QK_DIM + KV_DIM
QKV_SLABS = QKV_DIM // LANES
ATTN_SLABS = ATTN_DIM // LANES
KV_PAIRS = ATTN_KV_HEADS // 2
VMEM_LIMIT = 56 * 1024 * 1024

MAX_DIL = 16
SUPER = MAX_DIL * ATTN_BLOCK
IN_TM = 512
XBC_CHUNK = 512
SSD_STEP_CHUNKS = 4
TAIL_TM = 512
ROW_CLASSES = 8
FF_CHUNK = 256

assert DILATED_PATTERNS == ((128, 1), (512, 4), (2048, 16))
assert all(w // d == ATTN_BLOCK for w, d in DILATED_PATTERNS)


def _dot(a, b):
    return jnp.dot(a, b, preferred_element_type=F32)


def _dot_nt(a, b):
    return lax.dot_general(a, b, (((1,), (1,)), ((), ())), preferred_element_type=F32)


def _dot_tn(a, b):
    return lax.dot_general(a, b, (((0,), (0,)), ((), ())), preferred_element_type=F32)


def _rms(x):
    return x * lax.rsqrt(jnp.mean(x * x, axis=-1, keepdims=True) + EPS)


def _silu(x):
    return x * jax.nn.sigmoid(x)


def _const_spec(shape):
    nd = len(shape)
    return pl.BlockSpec(shape, lambda *_: (0,) * nd, pipeline_mode=pl.Buffered(1))


def _in_proj_kernel(tiles_per_batch,
                    x_ref, g_ref, wqk_ref, wv_ref, wz_ref, wxbc_ref, wdt_ref, qkg_ref, bd_ref,
                    cw_ref, cb_ref, dtb_ref,
                    qkv_ref, z_ref, xbc_ref, dt_ref, nat, mod4, *cbufs):
    i = pl.program_id(0)
    tm = x_ref.shape[0]

    @pl.when(i % tiles_per_batch == 0)
    def _():
        for cbuf in cbufs:
            cbuf[0:SUBLANES, :] = jnp.zeros((SUBLANES, XBC_CHUNK), F32)

    hb = (_rms(x_ref[...]) * g_ref[...]).astype(BF16)

    def ssm_conv(c):
        cbuf = cbufs[c]
        cs = slice(c * XBC_CHUNK, (c + 1) * XBC_CHUNK)
        cbuf[SUBLANES:SUBLANES + tm, :] = _dot(hb, wxbc_ref[:, cs])
        acc = cb_ref[:, cs] + cw_ref[SSM_CONV - 1:SSM_CONV, cs] * cbuf[SUBLANES:SUBLANES + tm, :]
        for kk in range(SSM_CONV - 1):
            off = SUBLANES - (SSM_CONV - 1) + kk
            acc = acc + cw_ref[kk:kk + 1, cs] * cbuf[off:off + tm, :]
        xbc_ref[:, cs] = _silu(acc)
        cbuf[0:SUBLANES, :] = cbuf[tm:tm + SUBLANES, :]

    def qk_norm():
        qk = _dot(hb, wqk_ref[...])
        sq = qk * qk
        sq_hi = sq.astype(BF16)
        sq_lo = (sq - sq_hi.astype(F32)).astype(BF16)
        bd = bd_ref[...]
        for c in range(QK_DIM // 256):
            sl = slice(c * 256, (c + 1) * 256)
            ss = _dot(sq_hi[:, sl], bd) + _dot(sq_lo[:, sl], bd)
            qkn = qk[:, sl] * lax.rsqrt(ss * (1.0 / HEAD_DIM) + EPS) * qkg_ref[:, sl]
            nat[2 * c] = qkn[:, :LANES]
            nat[2 * c + 1] = qkn[:, LANES:]

    def v_proj():
        v = _dot(hb, wv_ref[...])
        nat[QKV_SLABS - 2] = v[:, :LANES]
        nat[QKV_SLABS - 1] = v[:, LANES:]

    def relayout():
        quarter = tm // 4
        sixteenth = tm // MAX_DIL
        for s in range(QKV_SLABS):
            for a in range(4):
                mod4[s, a * quarter:(a + 1) * quarter, :] = nat[s, pl.ds(a, quarter, stride=4), :]
        for s in range(QKV_SLABS):
            for a in range(4):
                for b in range(4):
                    qkv_ref[s, 0, 4 * b + a, 0] = mod4[s, pl.ds(a * quarter + b, sixteenth, stride=4), :]

    def z_proj():
        z_ref[...] = _dot(hb, wz_ref[...]).astype(BF16)

    def dt_proj():
        dtr = _dot(hb, wdt_ref[...]) + dtb_ref[...]
        dt_ref[...] = jnp.maximum(dtr, 0.0) + jnp.log1p(jnp.exp(-jnp.abs(dtr)))

    for c in range(len(cbufs)):
        ssm_conv(c)
    qk_norm()
    v_proj()
    relayout()
    z_proj()
    dt_proj()


def _in_proj(x2d, g, wqk, wv, wz, wxbc, wdt, qkg, bd, cw, cb, dtb, seq):
    t = x2d.shape[0]
    tm = IN_TM
    assert SUPER % tm == 0 and tm % (4 * MAX_DIL) == 0 and seq % SUPER == 0
    grid = (t // tm,)
    row = lambda w: pl.BlockSpec((tm, w), lambda i: (i, 0))
    per_super = SUPER // tm
    qkv_shape = (QKV_SLABS, t // SUPER, MAX_DIL, per_super, tm // MAX_DIL, LANES)
    qkv, z, xbc, dt = pl.pallas_call(
        functools.partial(_in_proj_kernel, seq // tm),
        grid=grid,
        in_specs=[row(D_MODEL), _const_spec(g.shape), _const_spec(wqk.shape), _const_spec(wv.shape),
                  _const_spec(wz.shape), _const_spec(wxbc.shape), _const_spec(wdt.shape),
                  _const_spec(qkg.shape), _const_spec(bd.shape), _const_spec(cw.shape),
                  _const_spec(cb.shape), _const_spec(dtb.shape)],
        out_specs=[pl.BlockSpec((QKV_SLABS, 1, MAX_DIL, 1, tm // MAX_DIL, LANES),
                                lambda i: (0, i // per_super, 0, i % per_super, 0, 0)),
                   row(SSM_INNER), row(XBC_DIM), row(LANES)],
        out_shape=[jax.ShapeDtypeStruct(qkv_shape, F32), jax.ShapeDtypeStruct((t, SSM_INNER), BF16),
                   jax.ShapeDtypeStruct((t, XBC_DIM), F32), jax.ShapeDtypeStruct((t, LANES), F32)],
        scratch_shapes=[pltpu.VMEM((QKV_SLABS, tm, LANES), F32),
                        pltpu.VMEM((QKV_SLABS, tm, LANES), F32)]
                       + [pltpu.VMEM((tm + SUBLANES, XBC_CHUNK), F32)] * (XBC_DIM // XBC_CHUNK),
        compiler_params=pltpu.CompilerParams(dimension_semantics=("arbitrary",),
                                             vmem_limit_bytes=VMEM_LIMIT),
        name="in_proj",
    )(x2d, g, wqk, wv, wz, wxbc, wdt, qkg, bd, cw, cb, dtb)
    return qkv.reshape(QKV_SLABS, t, LANES), z, xbc, dt


def _block_perm(dil):
    reps = MAX_DIL // dil
    per = ATTN_BLOCK // reps
    pos = np.arange(ATTN_BLOCK)
    return (pos % per) * reps + pos // per


def _attn_bias():
    out = []
    for _, dil in DILATED_PATTERNS[::-1]:
        perm = _block_perm(dil)
        qi = perm[:, None]
        kj = np.concatenate([perm, ATTN_BLOCK + perm])[None, :]
        delta = ATTN_BLOCK + qi - kj
        band = (delta >= 0) & (delta <= ATTN_BLOCK)
        for first in (False, True):
            ok = band & (kj >= ATTN_BLOCK) if first else band
            out.append(np.tile(np.where(ok, 0.0, -np.inf).astype(np.float32), (4, 1)))
    return np.stack(out)


def _attn_kernel(cur_ref, prev_ref, bias_ref, out_ref, acc_o, acc_m, acc_l):
    first = (pl.program_id(1) == 0).astype(jnp.int32)
    lane = lax.broadcasted_iota(jnp.int32, (ATTN_BLOCK, LANES), 1)
    lo_half = lane < HEAD_DIM
    ones = jnp.ones((2 * ATTN_BLOCK, LANES), BF16)
    k_slab, v_slab = ATTN_SLABS, ATTN_SLABS + KV_PAIRS

    def rows(ref, slab, pieces):
        parts = [ref[slab, pl.ds(st, sz), :] for st, sz in pieces]
        return parts[0] if len(parts) == 1 else jnp.concatenate(parts, axis=0)

    def unit(q_pieces, prev_pieces, prev_in_cur, bias_idx, mode, out_start=None):
        bias = bias_ref[bias_idx]
        for p in range(KV_PAIRS):
            if prev_in_cur:
                kp, vp = rows(cur_ref, k_slab + p, prev_pieces), rows(cur_ref, v_slab + p, prev_pieces)
            else:
                kp, vp = rows(prev_ref, p, prev_pieces), rows(prev_ref, KV_PAIRS + p, prev_pieces)
            k2 = jnp.concatenate([kp, rows(cur_ref, k_slab + p, q_pieces)], axis=0).astype(BF16)
            v2 = jnp.concatenate([vp, rows(cur_ref, v_slab + p, q_pieces)], axis=0).astype(BF16)
            v2 = jnp.concatenate([v2, ones], axis=1)
            qs = []
            for g in range(ATTN_GROUP):
                q = rows(cur_ref, g * KV_PAIRS + p, q_pieces)
                qs += [jnp.where(lo_half, q, 0.0), jnp.where(lo_half, 0.0, q)]
            q4 = jnp.concatenate(qs, axis=0).astype(BF16)
            s = _dot_nt(q4, k2) + bias
            m = jnp.max(s, axis=-1, keepdims=True)
            pr = jnp.exp2(s - m).astype(BF16)
            ol = _dot(pr, v2)
            for g in range(ATTN_GROUP):
                r0, r1 = (2 * g) * ATTN_BLOCK, (2 * g + 1) * ATTN_BLOCK
                o = jnp.where(lo_half, ol[r0:r0 + ATTN_BLOCK, :LANES], ol[r1:r1 + ATTN_BLOCK, :LANES])
                l = jnp.where(lo_half, ol[r0:r0 + ATTN_BLOCK, LANES:], ol[r1:r1 + ATTN_BLOCK, LANES:])
                mm = jnp.where(lo_half, m[r0:r0 + ATTN_BLOCK], m[r1:r1 + ATTN_BLOCK])
                slab = g * KV_PAIRS + p
                if mode != "init":
                    m_old = rows(acc_m, slab, q_pieces)
                    m_new = jnp.maximum(m_old, mm)
                    w_old, w_new = jnp.exp2(m_old - m_new), jnp.exp2(mm - m_new)
                    o = rows(acc_o, slab, q_pieces) * w_old + o * w_new
                    l = rows(acc_l, slab, q_pieces) * w_old + l * w_new
                    mm = m_new
                if mode == "final":
                    res = o / l
                    for r in range(MAX_DIL):
                        out_ref[slab, pl.ds(out_start + r, SUBLANES, stride=MAX_DIL), :] = (
                            res[r * SUBLANES:(r + 1) * SUBLANES])
                else:
                    off = 0
                    for st, sz in q_pieces:
                        acc_o[slab, pl.ds(st, sz), :] = o[off:off + sz]
                        acc_m[slab, pl.ds(st, sz), :] = mm[off:off + sz]
                        acc_l[slab, pl.ds(st, sz), :] = l[off:off + sz]
                        off += sz

    def aligned(x, m):
        return x if isinstance(x, int) else pl.multiple_of(x, m)

    def d16_body(r, carry):
        st = pl.multiple_of(r * ATTN_BLOCK, ATTN_BLOCK)
        unit([(st, ATTN_BLOCK)], [(st, ATTN_BLOCK)], False, first, "init")
        return carry
    lax.fori_loop(0, MAX_DIL, d16_body, 0, unroll=MAX_DIL)

    def d4_pieces(c, row0):
        return [(aligned((4 * a + c) * ATTN_BLOCK + row0, 32), 32) for a in range(4)]

    def d4_body(c, carry):
        unit(d4_pieces(c, 0), d4_pieces(c, ATTN_BLOCK - 32), False, 2 + first, "merge")

        for n in range(1, 4):
            unit(d4_pieces(c, 32 * n), d4_pieces(c, 32 * (n - 1)), True, 2, "merge")
        return carry
    lax.fori_loop(0, 4, d4_body, 0, unroll=4)

    def d1_pieces(row0):
        return [(aligned(r * ATTN_BLOCK + row0, SUBLANES), SUBLANES) for r in range(MAX_DIL)]

    unit(d1_pieces(0), d1_pieces(ATTN_BLOCK - SUBLANES), False, 4 + first, "final", 0)

    def d1_body(n, carry):
        unit(d1_pieces(SUBLANES * n), d1_pieces(SUBLANES * (n - 1)), True, 4, "final", n * ATTN_BLOCK)
        return carry
    lax.fori_loop(1, SUPER // ATTN_BLOCK, d1_body, 0, unroll=SUPER // ATTN_BLOCK - 1)


def _attention(qkv, bias, bsz, seq):
    t = bsz * seq
    nsb = seq // SUPER
    acc = pltpu.VMEM((ATTN_SLABS, SUPER, LANES), F32)
    return pl.pallas_call(
        _attn_kernel,
        grid=(bsz, nsb),
        in_specs=[pl.BlockSpec((QKV_SLABS, SUPER, LANES), lambda b, s: (0, b * nsb + s, 0)),
                  pl.BlockSpec((QKV_SLABS // 2, SUPER, LANES),
                               lambda b, s: (1, b * nsb + jnp.maximum(s - 1, 0), 0)),
                  _const_spec(bias.shape)],
        out_specs=pl.BlockSpec((ATTN_SLABS, SUPER, LANES), lambda b, s: (0, b * nsb + s, 0)),
        out_shape=jax.ShapeDtypeStruct((ATTN_SLABS, t, LANES), F32),
        scratch_shapes=[acc, acc, acc],
        compiler_params=pltpu.CompilerParams(dimension_semantics=("arbitrary", "arbitrary"),
                                             vmem_limit_bytes=VMEM_LIMIT),
        name="attention",
    )(qkv, qkv, bias)


def _ssd_kernel(xbc_ref, dt_ref, dtn_ref, z_ref, alog_ref, expand_ref, dskip_ref, ng_ref, out_ref,
                state_ref, y_ref, xde_ref, e_ref, *decay_refs):
    nsub = SSD_STEP_CHUNKS
    cur = [decay_refs[3 * k:3 * k + 3] for k in range(nsub)]
    nxt = [decay_refs[3 * (nsub + k):3 * (nsub + k) + 3] for k in range(nsub)]
    c = pl.program_id(1)
    ln = SSM_CHUNK

    lane = lax.broadcasted_iota(jnp.int32, (ln, LANES), 1)
    lo_half = lane < SSM_HEAD_DIM
    part = lane // SSM_HEADS
    row = lax.broadcasted_iota(jnp.int32, (ln, ln), 0)
    col = lax.broadcasted_iota(jnp.int32, (ln, ln), 1)
    causal = row >= col
    tril = causal.astype(BF16)

    def split3(v):
        hi = v.astype(BF16).astype(F32)
        mid = (v - hi).astype(BF16).astype(F32)
        lo = (v - hi) - mid
        return hi, mid, lo

    def pack3(v):
        hi, mid, lo = split3(v)
        return jnp.where(part == 0, hi, jnp.where(part == 1, mid, jnp.where(part == 2, lo, 0.0))).astype(BF16)

    def decay_chain(dt, dst):
        a = dt * (-LOG2E * jnp.exp(alog_ref[...]))
        a3 = jnp.concatenate(split3(a), axis=0).astype(BF16)
        acum = _dot(jnp.concatenate([tril, tril, tril], axis=1), a3)
        total = acum[ln - 1:ln, :]
        packed = jnp.concatenate([pack3(dt), pack3(jnp.exp2(acum)), pack3(jnp.exp2(total - acum))], axis=0)
        packed_ref, acum_ref, acumt_ref = dst
        packed_ref[...] = packed
        acum_ref[...] = acum
        acumt_ref[...] = acum.T

    @pl.when(c == 0)
    def _():
        state_ref[...] = jnp.zeros_like(state_ref)
        for k in range(nsub):
            decay_chain(dt_ref[k * ln:(k + 1) * ln, :], cur[k])

    for k in range(nsub):
        decay_chain(dtn_ref[k * ln:(k + 1) * ln, :], nxt[k])

    gw = SSM_INNER // SSM_GROUPS
    pairs = SSM_HEADS // SSM_GROUPS // 2
    for k in range(nsub):
        rows = slice(k * ln, (k + 1) * ln)
        e = e_ref.at[k]
        xde = xde_ref.at[k]
        bgs, cbs, y_offs = [], [], []
        for g in range(SSM_GROUPS):
            bg = xbc_ref[rows, SSM_INNER + g * SSM_STATE:SSM_INNER + (g + 1) * SSM_STATE].astype(BF16)
            cg = xbc_ref[rows, SSM_INNER + BC_DIM + g * SSM_STATE:
                         SSM_INNER + BC_DIM + (g + 1) * SSM_STATE].astype(BF16)
            bgs.append(bg)
            cbs.append(_dot_nt(cg, bg))
            y_offs.append(_dot(cg, state_ref[g].astype(BF16)))

        e[...] = _dot(cur[k][0][...], expand_ref[...])
        acum = cur[k][1][...]
        acum_t = cur[k][2][...]
        ssq = jnp.zeros((ln, LANES), F32)
        for g in range(SSM_GROUPS):
            for j in range(pairs):
                h0 = g * 2 * pairs + 2 * j
                ps = slice(h0 * SSM_HEAD_DIM, (h0 + 2) * SSM_HEAD_DIM)
                xs = xbc_ref[rows, ps]
                xdt = xs * e[0:ln, ps]
                ms = []
                for h in (h0, h0 + 1):
                    seg = acum[:, h:h + 1] - acum_t[h:h + 1, :]
                    ms.append(cbs[g] * jnp.exp2(jnp.where(causal, seg, -jnp.inf)))
                mm = jnp.concatenate(ms, axis=1).astype(BF16)
                rhs = jnp.concatenate([jnp.where(lo_half, xdt, 0.0), jnp.where(lo_half, 0.0, xdt)],
                                      axis=0).astype(BF16)
                y = (_dot(mm, rhs) + y_offs[g][:, j * LANES:(j + 1) * LANES] * e[ln:2 * ln, ps]
                     + dskip_ref[:, ps] * xs)
                y = y * _silu(z_ref[rows, ps].astype(F32))
                y_ref[rows, ps] = y
                ssq = ssq + y * y
                xde[:, ps] = (xdt * e[2 * ln:3 * ln, ps]).astype(BF16)
            gs = slice(g * gw, (g + 1) * gw)
            state_ref[g] = state_ref[g] * e[2 * ln - 1:2 * ln, gs] + _dot_tn(bgs[g], xde[:, gs])

        scale = lax.rsqrt(jnp.sum(ssq, axis=-1, keepdims=True) * (1.0 / SSM_INNER) + EPS)
        out_ref[rows, :] = (y_ref[rows, :] * scale * ng_ref[...]).astype(BF16)

    for k in range(nsub):
        for dst, src in zip(cur[k], nxt[k]):
            dst[...] = src[...]


def _ssd(xbc, dt, z, alog, expand, dskip, ng, bsz, seq):
    t = bsz * seq
    nsub = SSD_STEP_CHUNKS
    rows = nsub * SSM_CHUNK
    assert seq % rows == 0
    ns = seq // rows
    row = lambda w: pl.BlockSpec((rows, w), lambda b, c: (b * ns + c, 0))
    nxt = pl.BlockSpec((rows, LANES), lambda b, c: (b * ns + jnp.minimum(c + 1, ns - 1), 0))
    decay_scratch = [pltpu.VMEM((3 * SSM_CHUNK, LANES), BF16), pltpu.VMEM((SSM_CHUNK, LANES), F32),
                     pltpu.VMEM((LANES, SSM_CHUNK), F32)]
    return pl.pallas_call(
        _ssd_kernel,
        grid=(bsz, ns),
        in_specs=[row(XBC_DIM), row(LANES), nxt, row(SSM_INNER), _const_spec(alog.shape),
                  _const_spec(expand.shape), _const_spec(dskip.shape), _const_spec(ng.shape)],
        out_specs=row(SSM_INNER),
        out_shape=jax.ShapeDtypeStruct((t, SSM_INNER), BF16),
        scratch_shapes=[pltpu.VMEM((SSM_GROUPS, SSM_STATE, SSM_INNER // SSM_GROUPS), F32),
                        pltpu.VMEM((rows, SSM_INNER), F32),
                        pltpu.VMEM((nsub, SSM_CHUNK, SSM_INNER), BF16),
                        pltpu.VMEM((nsub, 3 * SSM_CHUNK, SSM_INNER), F32)] + decay_scratch * (2 * nsub),
        compiler_params=pltpu.CompilerParams(dimension_semantics=("arbitrary", "arbitrary"),
                                             vmem_limit_bytes=VMEM_LIMIT),
        name="ssd",
    )(xbc, dt, dt, z, alog, expand, dskip, ng)


def _tail_kernel(tiles_per_batch,
                 x_ref, attn_ref, ssm_ref, p_ref, woa_ref, wos_ref, fg_ref, wup_ref, fcw_ref, fcb_ref, wdn_ref,
                 pg_ref, wpg_ref, wpp_ref, out_ref, slabs, act_ref, carry, sbuf):
    i = pl.program_id(0)
    tm = x_ref.shape[0]
    n8 = tm // ROW_CLASSES
    nslab = D_MODEL // LANES

    @pl.when(i % tiles_per_batch == 0)
    def _():
        carry[...] = jnp.zeros_like(carry)

    attn = jnp.concatenate([attn_ref[s] for s in range(ATTN_SLABS)], axis=1).astype(BF16)
    x1 = x_ref[...] + _dot(ssm_ref[...], wos_ref[...]) + _dot(attn, woa_ref[...])

    pp = _dot(p_ref[...].astype(BF16), wpp_ref[...])

    h2 = _rms(x1) * fg_ref[...]
    for s in range(nslab):
        slabs[s] = h2[:, s * LANES:(s + 1) * LANES]
    h2p = jnp.concatenate(
        [jnp.concatenate([slabs[s, pl.ds(q, n8, stride=ROW_CLASSES), :] for q in range(ROW_CLASSES)], axis=0)
         for s in range(nslab)], axis=1).astype(BF16)

    def conv(u, cs, slot):
        blk = [u[q * n8:(q + 1) * n8] for q in range(ROW_CLASSES)]
        shifted = []
        for k in range(FFN_CONV - 1):
            sb = sbuf.at[slot * (FFN_CONV - 1) + k]
            sb[0:SUBLANES, :] = carry[k, :, cs]
            sb[SUBLANES:SUBLANES + n8, :] = blk[ROW_CLASSES - (FFN_CONV - 1) + k]
            shifted.append(sb[SUBLANES - 1:SUBLANES - 1 + n8, :])
            carry[k, :, cs] = sb[n8:n8 + SUBLANES, :]
        prev = shifted + blk
        w = [fcw_ref[k:k + 1, cs] for k in range(FFN_CONV)]
        b = fcb_ref[:, cs]
        return jnp.concatenate([b + w[2] * prev[q + 2] + w[1] * prev[q + 1] + w[0] * prev[q]
                                for q in range(ROW_CLASSES)], axis=0)

    for c in range(D_FF // FF_CHUNK):
        gs = slice(c * FF_CHUNK, (c + 1) * FF_CHUNK)
        vs = slice(D_FF + c * FF_CHUNK, D_FF + (c + 1) * FF_CHUNK)
        gate = conv(_dot(h2p, wup_ref[:, gs]), gs, 0)
        val = conv(_dot(h2p, wup_ref[:, vs]), vs, 1)
        act_ref[:, gs] = (_silu(gate) * val).astype(BF16)
    ffn_p = _dot(act_ref[...], wdn_ref[...])

    for s in range(nslab):
        for q in range(ROW_CLASSES):
            slabs[s, pl.ds(q, n8, stride=ROW_CLASSES), :] = ffn_p[q * n8:(q + 1) * n8, s * LANES:(s + 1) * LANES]
    x2 = x1 + jnp.concatenate([slabs[s] for s in range(nslab)], axis=1)

    h3 = (_rms(x2) * pg_ref[...]).astype(BF16)
    gate = jax.nn.sigmoid(_dot(h3, wpg_ref[...]))
    out_ref[...] = x2 + gate * pp


def _tail(x2d, attn, ssm, p2d, woa, wos, fg, wup, fcw, fcb, wdn, pg, wpg, wpp, seq):
    t = x2d.shape[0]
    tm = TAIL_TM
    assert seq % tm == 0 and tm % (ROW_CLASSES * SUBLANES) == 0 and D_FF % FF_CHUNK == 0
    row = lambda w: pl.BlockSpec((tm, w), lambda i: (i, 0))
    consts = [woa, wos, fg, wup, fcw, fcb, wdn, pg, wpg, wpp]
    return pl.pallas_call(
        functools.partial(_tail_kernel, seq // tm),
        grid=(t // tm,),
        in_specs=[row(D_MODEL), pl.BlockSpec((ATTN_SLABS, tm, LANES), lambda i: (0, i, 0)),
                  row(SSM_INNER), row(PLE_DIM)] + [_const_spec(w.shape) for w in consts],
        out_specs=row(D_MODEL),
        out_shape=jax.ShapeDtypeStruct((t, D_MODEL), F32),
        scratch_shapes=[pltpu.VMEM((D_MODEL // LANES, tm, LANES), F32),
                        pltpu.VMEM((tm, D_FF), BF16),
                        pltpu.VMEM((FFN_CONV - 1, SUBLANES, 2 * D_FF), F32),
                        pltpu.VMEM((2 * (FFN_CONV - 1), SUBLANES + tm // ROW_CLASSES, FF_CHUNK), F32)],
        compiler_params=pltpu.CompilerParams(dimension_semantics=("arbitrary",),
                                             vmem_limit_bytes=VMEM_LIMIT),
        name="tail",
    )(x2d, attn, ssm, p2d, *consts)


def _q_perm():
    idx = np.arange(ATTN_DIM).reshape(ATTN_KV_HEADS, ATTN_GROUP, HEAD_DIM)
    return np.transpose(idx, (1, 0, 2)).reshape(-1)


def _layer(x, p, attn_norm_g, w_in, q_norm_g, k_norm_g, ssm_conv_w, ssm_conv_b, dt_bias, a_log,
           d_skip, ssm_norm_g, w_out, ffn_norm_g, w_up, ffn_conv_w, ffn_conv_b, w_down,
           ple_norm_g, w_ple_gate, w_ple_proj):
    bsz, seq, _ = x.shape
    t = bsz * seq
    x2d = x.reshape(t, D_MODEL)
    p2d = p.reshape(t, PLE_DIM)
    qperm = _q_perm()

    o_k, o_v, o_z = ATTN_DIM, ATTN_DIM + KV_DIM, ATTN_DIM + 2 * KV_DIM
    o_xbc, o_dt = o_z + SSM_INNER, o_z + SSM_INNER + XBC_DIM
    wqk = jnp.concatenate([w_in[:, :ATTN_DIM][:, qperm], w_in[:, o_k:o_v]], axis=1).astype(BF16)
    wv = w_in[:, o_v:o_z].astype(BF16)
    wz = w_in[:, o_z:o_xbc].astype(BF16)
    wxbc = w_in[:, o_xbc:o_dt].astype(BF16)
    head_reps = LANES // SSM_HEADS
    wdt = jnp.tile(w_in[:, o_dt:], (1, head_reps)).astype(BF16)
    qkg = jnp.concatenate([jnp.tile(q_norm_g * (HEAD_DIM ** -0.5 * LOG2E), ATTN_Q_HEADS),
                           jnp.tile(k_norm_g, ATTN_KV_HEADS)]).reshape(1, QK_DIM)
    hid = np.arange(256) // HEAD_DIM
    bd = jnp.asarray(hid[:, None] == hid[None, :], BF16)
    dtb = jnp.tile(dt_bias, head_reps).reshape(1, LANES)
    alog = jnp.tile(a_log, head_reps).reshape(1, LANES)

    qkv, z, xbc, dt = _in_proj(x2d, attn_norm_g.reshape(1, D_MODEL), wqk, wv, wz, wxbc, wdt, qkg, bd,
                               ssm_conv_w, ssm_conv_b.reshape(1, XBC_DIM), dtb, seq)

    attn = _attention(qkv, jnp.asarray(_attn_bias()), bsz, seq)

    src_lane = np.arange(LANES)[:, None]
    expand = jnp.asarray((src_lane % SSM_HEADS == (np.arange(SSM_INNER) // SSM_HEAD_DIM)[None, :])
                         & (src_lane < 3 * SSM_HEADS), BF16)
    dskip = jnp.repeat(d_skip, SSM_HEAD_DIM).reshape(1, SSM_INNER)
    ssm = _ssd(xbc, dt, z, alog, expand, dskip, ssm_norm_g.reshape(1, SSM_INNER), bsz, seq)

    woa = w_out[:ATTN_DIM][qperm].astype(BF16)
    wos = w_out[ATTN_DIM:].astype(BF16)
    out = _tail(x2d, attn, ssm, p2d, woa, wos, ffn_norm_g.reshape(1, D_MODEL), w_up.astype(BF16),
                ffn_conv_w, ffn_conv_b.reshape(1, 2 * D_FF), w_down.astype(BF16),
                ple_norm_g.reshape(1, D_MODEL), w_ple_gate.astype(BF16), w_ple_proj.astype(BF16), seq)
    return out.reshape(bsz, seq, D_MODEL)


def kernel(x, p, attn_norm_g, w_in, q_norm_g, k_norm_g, ssm_conv_w, ssm_conv_b, dt_bias, a_log, d_skip,
           ssm_norm_g, w_out, ffn_norm_g, w_up, ffn_conv_w, ffn_conv_b, w_down, ple_norm_g, w_ple_gate,
           w_ple_proj):
    depth = w_in.shape[0]
    for i in range(depth):
        x = _layer(x, p[i], attn_norm_g[i], w_in[i], q_norm_g[i], k_norm_g[i], ssm_conv_w[i], ssm_conv_b[i],
                   dt_bias[i], a_log[i], d_skip[i], ssm_norm_g[i], w_out[i], ffn_norm_g[i], w_up[i],
                   ffn_conv_w[i], ffn_conv_b[i], w_down[i], ple_norm_g[i], w_ple_gate[i], w_ple_proj[i])
    return x
```

```python
import functools

import jax
import jax.numpy as jnp
import numpy as np
from jax import lax
from jax.experimental import pallas as pl
from jax.experimental.pallas import tpu as pltpu

F32 = jnp.float32
BF16 = jnp.bfloat16

D_MODEL = 1024
HEAD_DIM = 64
ATTN_Q_HEADS = 8
ATTN_KV_HEADS = 4
ATTN_GROUP = ATTN_Q_HEADS // ATTN_KV_HEADS
DILATED_PATTERNS = ((128, 1), (512, 4), (2048, 16))
ATTN_BLOCK = 128
ATTN_DIM = ATTN_Q_HEADS * HEAD_DIM
KV_DIM = ATTN_KV_HEADS * HEAD_DIM
SSM_HEADS = 16
SSM_HEAD_DIM = 64
SSM_INNER = SSM_HEADS * SSM_HEAD_DIM
SSM_GROUPS = 2
SSM_STATE = 128
SSM_CONV = 4
SSM_CHUNK = 128
BC_DIM = SSM_GROUPS * SSM_STATE
XBC_DIM = SSM_INNER + 2 * BC_DIM
D_FF = 2816
FFN_CONV = 3
PLE_DIM = 256
EPS = 1e-6
LOG2E = 1.4426950408889634

LANES = 128
SUBLANES = 8
QK_DIM = ATTN_DIM + KV_DIM
QKV_DIM = QK_DIM + KV_DIM
COL_K = ATTN_DIM
COL_V = COL_K + KV_DIM
COL_Z = COL_V + KV_DIM
COL_XBC = COL_Z + SSM_INNER
COL_DT = COL_XBC + XBC_DIM
QKV_SLABS = QKV_DIM // LANES
ATTN_SLABS = ATTN_DIM // LANES
KV_PAIRS = ATTN_KV_HEADS // 2
VMEM_LIMIT = 56 * 1024 * 1024

MAX_DIL = 16
SUPER = MAX_DIL * ATTN_BLOCK
IN_TM = 512
XBC_CHUNK = 512
SSD_STEP_CHUNKS = 4
TAIL_TM = 512
ROW_CLASSES = 8
FF_CHUNK = 256

assert DILATED_PATTERNS == ((128, 1), (512, 4), (2048, 16))
assert all(w // d == ATTN_BLOCK for w, d in DILATED_PATTERNS)


def _dot(a, b):
    return jnp.dot(a, b, preferred_element_type=F32)


def _dot_nt(a, b):
    return lax.dot_general(a, b, (((1,), (1,)), ((), ())), preferred_element_type=F32)


def _dot_tn(a, b):
    return lax.dot_general(a, b, (((0,), (0,)), ((), ())), preferred_element_type=F32)


def _rms(x):
    return x * lax.rsqrt(jnp.mean(x * x, axis=-1, keepdims=True) + EPS)


def _silu(x):
    return x * jax.nn.sigmoid(x)


def _const_spec(shape):
    nd = len(shape)
    return pl.BlockSpec(shape, lambda *_: (0,) * nd, pipeline_mode=pl.Buffered(1))


def _in_proj_kernel(tiles_per_batch,
                    x_ref, g_ref, wq_ref, win_ref, wdt_ref, qkg_ref, bd_ref,
                    cw_ref, cb_ref, dtb_ref,
                    qkv_ref, z_ref, xbc_ref, dt_ref, nat, mod4, *cbufs):
    i = pl.program_id(0)
    tm = x_ref.shape[0]

    @pl.when(i % tiles_per_batch == 0)
    def _():
        for cbuf in cbufs:
            cbuf[0:SUBLANES, :] = jnp.zeros((SUBLANES, XBC_CHUNK), F32)

    hb = (_rms(x_ref[...]) * g_ref[...]).astype(BF16)

    def ssm_conv(c):
        cbuf = cbufs[c]
        cs = slice(c * XBC_CHUNK, (c + 1) * XBC_CHUNK)
        cbuf[SUBLANES:SUBLANES + tm, :] = _dot(hb, win_ref[:, COL_XBC + c * XBC_CHUNK:COL_XBC + (c + 1) * XBC_CHUNK])
        acc = cb_ref[:, cs] + cw_ref[SSM_CONV - 1:SSM_CONV, cs] * cbuf[SUBLANES:SUBLANES + tm, :]
        for kk in range(SSM_CONV - 1):
            off = SUBLANES - (SSM_CONV - 1) + kk
            acc = acc + cw_ref[kk:kk + 1, cs] * cbuf[off:off + tm, :]
        xbc_ref[:, cs] = _silu(acc)
        cbuf[0:SUBLANES, :] = cbuf[tm:tm + SUBLANES, :]

    def qk_norm():
        qk = jnp.concatenate([_dot(hb, wq_ref[...]), _dot(hb, win_ref[:, COL_K:COL_V])], axis=1)
        sq = qk * qk
        sq_hi = sq.astype(BF16)
        sq_lo = (sq - sq_hi.astype(F32)).astype(BF16)
        bd = bd_ref[...]
        for c in range(QK_DIM // 256):
            sl = slice(c * 256, (c + 1) * 256)
            ss = _dot(sq_hi[:, sl], bd) + _dot(sq_lo[:, sl], bd)
            qkn = qk[:, sl] * lax.rsqrt(ss * (1.0 / HEAD_DIM) + EPS) * qkg_ref[:, sl]
            nat[2 * c] = qkn[:, :LANES]
            nat[2 * c + 1] = qkn[:, LANES:]

    def v_proj():
        v = _dot(hb, win_ref[:, COL_V:COL_Z])
        nat[QKV_SLABS - 2] = v[:, :LANES]
        nat[QKV_SLABS - 1] = v[:, LANES:]

    def relayout():
        quarter = tm // 4
        sixteenth = tm // MAX_DIL
        for s in range(QKV_SLABS):
            for a in range(4):
                mod4[s, a * quarter:(a + 1) * quarter, :] = nat[s, pl.ds(a, quarter, stride=4), :]
        for s in range(QKV_SLABS):
            for a in range(4):
                for b in range(4):
                    qkv_ref[s, 0, 4 * b + a, 0] = mod4[s, pl.ds(a * quarter + b, sixteenth, stride=4), :]

    def z_proj():
        z_ref[...] = _dot(hb, win_ref[:, COL_Z:COL_XBC]).astype(BF16)

    def dt_proj():
        dtr = _dot(hb, wdt_ref[...]) + dtb_ref[...]
        dt_ref[...] = jnp.maximum(dtr, 0.0) + jnp.log1p(jnp.exp(-jnp.abs(dtr)))

    for c in range(len(cbufs)):
        ssm_conv(c)
    qk_norm()
    v_proj()
    relayout()
    z_proj()
    dt_proj()


def _in_proj(x2d, g, wq, win, wdt, qkg, bd, cw, cb, dtb, seq):
    t = x2d.shape[0]
    tm = IN_TM
    assert SUPER % tm == 0 and tm % (4 * MAX_DIL) == 0 and seq % SUPER == 0
    grid = (t // tm,)
    row = lambda w: pl.BlockSpec((tm, w), lambda i: (i, 0))
    per_super = SUPER // tm
    qkv_shape = (QKV_SLABS, t // SUPER, MAX_DIL, per_super, tm // MAX_DIL, LANES)
    qkv, z, xbc, dt = pl.pallas_call(
        functools.partial(_in_proj_kernel, seq // tm),
        grid=grid,
        in_specs=[row(D_MODEL), _const_spec(g.shape), _const_spec(wq.shape), _const_spec(win.shape),
                  _const_spec(wdt.shape),
                  _const_spec(qkg.shape), _const_spec(bd.shape), _const_spec(cw.shape),
                  _const_spec(cb.shape), _const_spec(dtb.shape)],
        out_specs=[pl.BlockSpec((QKV_SLABS, 1, MAX_DIL, 1, tm // MAX_DIL, LANES),
                                lambda i: (0, i // per_super, 0, i % per_super, 0, 0)),
                   row(SSM_INNER), row(XBC_DIM), row(LANES)],
        out_shape=[jax.ShapeDtypeStruct(qkv_shape, F32), jax.ShapeDtypeStruct((t, SSM_INNER), BF16),
                   jax.ShapeDtypeStruct((t, XBC_DIM), F32), jax.ShapeDtypeStruct((t, LANES), F32)],
        scratch_shapes=[pltpu.VMEM((QKV_SLABS, tm, LANES), F32),
                        pltpu.VMEM((QKV_SLABS, tm, LANES), F32)]
                       + [pltpu.VMEM((tm + SUBLANES, XBC_CHUNK), F32)] * (XBC_DIM // XBC_CHUNK),
        compiler_params=pltpu.CompilerParams(dimension_semantics=("arbitrary",),
                                             vmem_limit_bytes=VMEM_LIMIT),
        name="in_proj",
    )(x2d, g, wq, win, wdt, qkg, bd, cw, cb, dtb)
    return qkv.reshape(QKV_SLABS, t, LANES), z, xbc, dt


def _block_perm(dil):
    reps = MAX_DIL // dil
    per = ATTN_BLOCK // reps
    pos = np.arange(ATTN_BLOCK)
    return (pos % per) * reps + pos // per


def _attn_bias():
    out = []
    for _, dil in DILATED_PATTERNS[::-1]:
        perm = _block_perm(dil)
        qi = perm[:, None]
        kj = np.concatenate([perm, ATTN_BLOCK + perm])[None, :]
        delta = ATTN_BLOCK + qi - kj
        band = (delta >= 0) & (delta <= ATTN_BLOCK)
        for first in (False, True):
            ok = band & (kj >= ATTN_BLOCK) if first else band
            out.append(np.tile(np.where(ok, 0.0, -np.inf).astype(np.float32), (4, 1)))
    return np.stack(out)


def _attn_kernel(cur_ref, prev_ref, bias_ref, out_ref, acc_o, acc_m, acc_l):
    first = (pl.program_id(1) == 0).astype(jnp.int32)
    lane = lax.broadcasted_iota(jnp.int32, (ATTN_BLOCK, LANES), 1)
    lo_half = lane < HEAD_DIM
    ones = jnp.ones((2 * ATTN_BLOCK, LANES), BF16)
    k_slab, v_slab = ATTN_SLABS, ATTN_SLABS + KV_PAIRS

    def rows(ref, slab, pieces):
        parts = [ref[slab, pl.ds(st, sz), :] for st, sz in pieces]
        return parts[0] if len(parts) == 1 else jnp.concatenate(parts, axis=0)

    def unit(q_pieces, prev_pieces, prev_in_cur, bias_idx, mode, out_start=None):
        bias = bias_ref[bias_idx]
        for p in range(KV_PAIRS):
            if prev_in_cur:
                kp, vp = rows(cur_ref, k_slab + p, prev_pieces), rows(cur_ref, v_slab + p, prev_pieces)
            else:
                kp, vp = rows(prev_ref, p, prev_pieces), rows(prev_ref, KV_PAIRS + p, prev_pieces)
            k2 = jnp.concatenate([kp, rows(cur_ref, k_slab + p, q_pieces)], axis=0).astype(BF16)
            v2 = jnp.concatenate([vp, rows(cur_ref, v_slab + p, q_pieces)], axis=0).astype(BF16)
            v2 = jnp.concatenate([v2, ones], axis=1)
            qs = []
            for g in range(ATTN_GROUP):
                q = rows(cur_ref, g * KV_PAIRS + p, q_pieces)
                qs += [jnp.where(lo_half, q, 0.0), jnp.where(lo_half, 0.0, q)]
            q4 = jnp.concatenate(qs, axis=0).astype(BF16)
            s = _dot_nt(q4, k2) + bias
            m = jnp.max(s, axis=-1, keepdims=True)
            pr = jnp.exp2(s - m).astype(BF16)
            ol = _dot(pr, v2)
            for g in range(ATTN_GROUP):
                r0, r1 = (2 * g) * ATTN_BLOCK, (2 * g + 1) * ATTN_BLOCK
                o = jnp.where(lo_half, ol[r0:r0 + ATTN_BLOCK, :LANES], ol[r1:r1 + ATTN_BLOCK, :LANES])
                l = jnp.where(lo_half, ol[r0:r0 + ATTN_BLOCK, LANES:], ol[r1:r1 + ATTN_BLOCK, LANES:])
                mm = jnp.where(lo_half, m[r0:r0 + ATTN_BLOCK], m[r1:r1 + ATTN_BLOCK])
                slab = g * KV_PAIRS + p
                if mode != "init":
                    m_old = rows(acc_m, slab, q_pieces)
                    m_new = jnp.maximum(m_old, mm)
                    w_old, w_new = jnp.exp2(m_old - m_new), jnp.exp2(mm - m_new)
                    o = rows(acc_o, slab, q_pieces) * w_old + o * w_new
                    l = rows(acc_l, slab, q_pieces) * w_old + l * w_new
                    mm = m_new
                if mode == "final":
                    res = o / l
                    for r in range(MAX_DIL):
                        out_ref[slab, pl.ds(out_start + r, SUBLANES, stride=MAX_DIL), :] = (
                            res[r * SUBLANES:(r + 1) * SUBLANES])
                else:
                    off = 0
                    for st, sz in q_pieces:
                        acc_o[slab, pl.ds(st, sz), :] = o[off:off + sz]
                        acc_m[slab, pl.ds(st, sz), :] = mm[off:off + sz]
                        acc_l[slab, pl.ds(st, sz), :] = l[off:off + sz]
                        off += sz

    def aligned(x, m):
        return x if isinstance(x, int) else pl.multiple_of(x, m)

    def d16_body(r, carry):
        st = pl.multiple_of(r * ATTN_BLOCK, ATTN_BLOCK)
        unit([(st, ATTN_BLOCK)], [(st, ATTN_BLOCK)], False, first, "init")
        return carry
    lax.fori_loop(0, MAX_DIL, d16_body, 0, unroll=MAX_DIL)

    def d4_pieces(c, row0):
        return [(aligned((4 * a + c) * ATTN_BLOCK + row0, 32), 32) for a in range(4)]

    def d4_body(c, carry):
        unit(d4_pieces(c, 0), d4_pieces(c, ATTN_BLOCK - 32), False, 2 + first, "merge")

        for n in range(1, 4):
            unit(d4_pieces(c, 32 * n), d4_pieces(c, 32 * (n - 1)), True, 2, "merge")
        return carry
    lax.fori_loop(0, 4, d4_body, 0, unroll=4)

    def d1_pieces(row0):
        return [(aligned(r * ATTN_BLOCK + row0, SUBLANES), SUBLANES) for r in range(MAX_DIL)]

    unit(d1_pieces(0), d1_pieces(ATTN_BLOCK - SUBLANES), False, 4 + first, "final", 0)

    def d1_body(n, carry):
        unit(d1_pieces(SUBLANES * n), d1_pieces(SUBLANES * (n - 1)), True, 4, "final", n * ATTN_BLOCK)
        return carry
    lax.fori_loop(1, SUPER // ATTN_BLOCK, d1_body, 0, unroll=SUPER // ATTN_BLOCK - 1)


def _attention(qkv, bias, bsz, seq):
    t = bsz * seq
    nsb = seq // SUPER
    acc = pltpu.VMEM((ATTN_SLABS, SUPER, LANES), F32)
    return pl.pallas_call(
        _attn_kernel,
        grid=(bsz, nsb),
        in_specs=[pl.BlockSpec((QKV_SLABS, SUPER, LANES), lambda b, s: (0, b * nsb + s, 0)),
                  pl.BlockSpec((QKV_SLABS // 2, SUPER, LANES),
                               lambda b, s: (1, b * nsb + jnp.maximum(s - 1, 0), 0)),
                  _const_spec(bias.shape)],
        out_specs=pl.BlockSpec((ATTN_SLABS, SUPER, LANES), lambda b, s: (0, b * nsb + s, 0)),
        out_shape=jax.ShapeDtypeStruct((ATTN_SLABS, t, LANES), F32),
        scratch_shapes=[acc, acc, acc],
        compiler_params=pltpu.CompilerParams(dimension_semantics=("arbitrary", "arbitrary"),
                                             vmem_limit_bytes=VMEM_LIMIT),
        name="attention",
    )(qkv, qkv, bias)


def _ssd_kernel(xbc_ref, dt_ref, dtn_ref, z_ref, alog_ref, expand_ref, dskip_ref, ng_ref, out_ref,
                state_ref, y_ref, xde_ref, e_ref, *decay_refs):
    nsub = SSD_STEP_CHUNKS
    cur = [decay_refs[3 * k:3 * k + 3] for k in range(nsub)]
    nxt = [decay_refs[3 * (nsub + k):3 * (nsub + k) + 3] for k in range(nsub)]
    c = pl.program_id(1)
    ln = SSM_CHUNK

    lane = lax.broadcasted_iota(jnp.int32, (ln, LANES), 1)
    lo_half = lane < SSM_HEAD_DIM
    part = lane // SSM_HEADS
    row = lax.broadcasted_iota(jnp.int32, (ln, ln), 0)
    col = lax.broadcasted_iota(jnp.int32, (ln, ln), 1)
    causal = row >= col
    tril = causal.astype(BF16)

    def split3(v):
        hi = v.astype(BF16).astype(F32)
        mid = (v - hi).astype(BF16).astype(F32)
        lo = (v - hi) - mid
        return hi, mid, lo

    def pack3(v):
        hi, mid, lo = split3(v)
        return jnp.where(part == 0, hi, jnp.where(part == 1, mid, jnp.where(part == 2, lo, 0.0))).astype(BF16)

    def decay_chain(dt, dst):
        a = dt * (-LOG2E * jnp.exp(alog_ref[...]))
        a3 = jnp.concatenate(split3(a), axis=0).astype(BF16)
        acum = _dot(jnp.concatenate([tril, tril, tril], axis=1), a3)
        total = acum[ln - 1:ln, :]
        packed = jnp.concatenate([pack3(dt), pack3(jnp.exp2(acum)), pack3(jnp.exp2(total - acum))], axis=0)
        packed_ref, acum_ref, acumt_ref = dst
        packed_ref[...] = packed
        acum_ref[...] = acum
        acumt_ref[...] = acum.T

    @pl.when(c == 0)
    def _():
        state_ref[...] = jnp.zeros_like(state_ref)
        for k in range(nsub):
            decay_chain(dt_ref[k * ln:(k + 1) * ln, :], cur[k])

    for k in range(nsub):
        decay_chain(dtn_ref[k * ln:(k + 1) * ln, :], nxt[k])

    gw = SSM_INNER // SSM_GROUPS
    pairs = SSM_HEADS // SSM_GROUPS // 2
    for k in range(nsub):
        rows = slice(k * ln, (k + 1) * ln)
        e = e_ref.at[k]
        xde = xde_ref.at[k]
        bgs, cbs, y_offs = [], [], []
        for g in range(SSM_GROUPS):
            bg = xbc_ref[rows, SSM_INNER + g * SSM_STATE:SSM_INNER + (g + 1) * SSM_STATE].astype(BF16)
            cg = xbc_ref[rows, SSM_INNER + BC_DIM + g * SSM_STATE:
                         SSM_INNER + BC_DIM + (g + 1) * SSM_STATE].astype(BF16)
            bgs.append(bg)
            cbs.append(_dot_nt(cg, bg))
            y_offs.append(_dot(cg, state_ref[g].astype(BF16)))

        e[...] = _dot(cur[k][0][...], expand_ref[...])
        acum = cur[k][1][...]
        acum_t = cur[k][2][...]
        ssq = jnp.zeros((ln, LANES), F32)
        for g in range(SSM_GROUPS):
            for j in range(pairs):
                h0 = g * 2 * pairs + 2 * j
                ps = slice(h0 * SSM_HEAD_DIM, (h0 + 2) * SSM_HEAD_DIM)
                xs = xbc_ref[rows, ps]
                xdt = xs * e[0:ln, ps]
                ms = []
                for h in (h0, h0 + 1):
                    seg = acum[:, h:h + 1] - acum_t[h:h + 1, :]
                    ms.append(cbs[g] * jnp.exp2(jnp.where(causal, seg, -jnp.inf)))
                mm = jnp.concatenate(ms, axis=1).astype(BF16)
                rhs = jnp.concatenate([jnp.where(lo_half, xdt, 0.0), jnp.where(lo_half, 0.0, xdt)],
                                      axis=0).astype(BF16)
                y = (_dot(mm, rhs) + y_offs[g][:, j * LANES:(j + 1) * LANES] * e[ln:2 * ln, ps]
                     + dskip_ref[:, ps] * xs)
                y = y * _silu(z_ref[rows, ps].astype(F32))
                y_ref[rows, ps] = y
                ssq = ssq + y * y
                xde[:, ps] = (xdt * e[2 * ln:3 * ln, ps]).astype(BF16)
            gs = slice(g * gw, (g + 1) * gw)
            state_ref[g] = state_ref[g] * e[2 * ln - 1:2 * ln, gs] + _dot_tn(bgs[g], xde[:, gs])

        scale = lax.rsqrt(jnp.sum(ssq, axis=-1, keepdims=True) * (1.0 / SSM_INNER) + EPS)
        out_ref[rows, :] = (y_ref[rows, :] * scale * ng_ref[...]).astype(BF16)

    for k in range(nsub):
        for dst, src in zip(cur[k], nxt[k]):
            dst[...] = src[...]


def _ssd(xbc, dt, z, alog, expand, dskip, ng, bsz, seq):
    t = bsz * seq
    nsub = SSD_STEP_CHUNKS
    rows = nsub * SSM_CHUNK
    assert seq % rows == 0
    ns = seq // rows
    row = lambda w: pl.BlockSpec((rows, w), lambda b, c: (b * ns + c, 0))
    nxt = pl.BlockSpec((rows, LANES), lambda b, c: (b * ns + jnp.minimum(c + 1, ns - 1), 0))
    decay_scratch = [pltpu.VMEM((3 * SSM_CHUNK, LANES), BF16), pltpu.VMEM((SSM_CHUNK, LANES), F32),
                     pltpu.VMEM((LANES, SSM_CHUNK), F32)]
    return pl.pallas_call(
        _ssd_kernel,
        grid=(bsz, ns),
        in_specs=[row(XBC_DIM), row(LANES), nxt, row(SSM_INNER), _const_spec(alog.shape),
                  _const_spec(expand.shape), _const_spec(dskip.shape), _const_spec(ng.shape)],
        out_specs=row(SSM_INNER),
        out_shape=jax.ShapeDtypeStruct((t, SSM_INNER), BF16),
        scratch_shapes=[pltpu.VMEM((SSM_GROUPS, SSM_STATE, SSM_INNER // SSM_GROUPS), F32),
                        pltpu.VMEM((rows, SSM_INNER), F32),
                        pltpu.VMEM((nsub, SSM_CHUNK, SSM_INNER), BF16),
                        pltpu.VMEM((nsub, 3 * SSM_CHUNK, SSM_INNER), F32)] + decay_scratch * (2 * nsub),
        compiler_params=pltpu.CompilerParams(dimension_semantics=("arbitrary", "arbitrary"),
                                             vmem_limit_bytes=VMEM_LIMIT),
        name="ssd",
    )(xbc, dt, dt, z, alog, expand, dskip, ng)


def _tail_kernel(tiles_per_batch,
                 x_ref, attn_ref, ssm_ref, p_ref, woa_ref, wos_ref, fg_ref, wup_ref, fcw_ref, fcb_ref, wdn_ref,
                 pg_ref, wpg_ref, wpp_ref, out_ref, slabs, act_ref, carry, sbuf):
    i = pl.program_id(0)
    tm = x_ref.shape[0]
    n8 = tm // ROW_CLASSES
    nslab = D_MODEL // LANES

    @pl.when(i % tiles_per_batch == 0)
    def _():
        carry[...] = jnp.zeros_like(carry)

    attn = jnp.concatenate([attn_ref[s] for s in range(ATTN_SLABS)], axis=1).astype(BF16)
    x1 = x_ref[...] + _dot(ssm_ref[...], wos_ref[ATTN_DIM:, :]) + _dot(attn, woa_ref[...])

    pp = _dot(p_ref[...].astype(BF16), wpp_ref[...])

    h2 = _rms(x1) * fg_ref[...]
    for s in range(nslab):
        slabs[s] = h2[:, s * LANES:(s + 1) * LANES]
    h2p = jnp.concatenate(
        [jnp.concatenate([slabs[s, pl.ds(q, n8, stride=ROW_CLASSES), :] for q in range(ROW_CLASSES)], axis=0)
         for s in range(nslab)], axis=1).astype(BF16)

    def conv(u, cs, slot):
        blk = [u[q * n8:(q + 1) * n8] for q in range(ROW_CLASSES)]
        shifted = []
        for k in range(FFN_CONV - 1):
            sb = sbuf.at[slot * (FFN_CONV - 1) + k]
            sb[0:SUBLANES, :] = carry[k, :, cs]
            sb[SUBLANES:SUBLANES + n8, :] = blk[ROW_CLASSES - (FFN_CONV - 1) + k]
            shifted.append(sb[SUBLANES - 1:SUBLANES - 1 + n8, :])
            carry[k, :, cs] = sb[n8:n8 + SUBLANES, :]
        prev = shifted + blk
        w = [fcw_ref[k:k + 1, cs] for k in range(FFN_CONV)]
        b = fcb_ref[:, cs]
        return jnp.concatenate([b + w[2] * prev[q + 2] + w[1] * prev[q + 1] + w[0] * prev[q]
                                for q in range(ROW_CLASSES)], axis=0)

    for c in range(D_FF // FF_CHUNK):
        gs = slice(c * FF_CHUNK, (c + 1) * FF_CHUNK)
        vs = slice(D_FF + c * FF_CHUNK, D_FF + (c + 1) * FF_CHUNK)
        gate = conv(_dot(h2p, wup_ref[:, gs]), gs, 0)
        val = conv(_dot(h2p, wup_ref[:, vs]), vs, 1)
        act_ref[:, gs] = (_silu(gate) * val).astype(BF16)
    ffn_p = _dot(act_ref[...], wdn_ref[...])

    for s in range(nslab):
        for q in range(ROW_CLASSES):
            slabs[s, pl.ds(q, n8, stride=ROW_CLASSES), :] = ffn_p[q * n8:(q + 1) * n8, s * LANES:(s + 1) * LANES]
    x2 = x1 + jnp.concatenate([slabs[s] for s in range(nslab)], axis=1)

    h3 = (_rms(x2) * pg_ref[...]).astype(BF16)
    gate = jax.nn.sigmoid(_dot(h3, wpg_ref[...]))
    out_ref[...] = x2 + gate * pp


def _tail(x2d, attn, ssm, p2d, woa, wos, fg, wup, fcw, fcb, wdn, pg, wpg, wpp, seq):
    t = x2d.shape[0]
    tm = TAIL_TM
    assert seq % tm == 0 and tm % (ROW_CLASSES * SUBLANES) == 0 and D_FF % FF_CHUNK == 0
    row = lambda w: pl.BlockSpec((tm, w), lambda i: (i, 0))
    consts = [woa, wos, fg, wup, fcw, fcb, wdn, pg, wpg, wpp]
    return pl.pallas_call(
        functools.partial(_tail_kernel, seq // tm),
        grid=(t // tm,),
        in_specs=[row(D_MODEL), pl.BlockSpec((ATTN_SLABS, tm, LANES), lambda i: (0, i, 0)),
                  row(SSM_INNER), row(PLE_DIM)] + [_const_spec(w.shape) for w in consts],
        out_specs=row(D_MODEL),
        out_shape=jax.ShapeDtypeStruct((t, D_MODEL), F32),
        scratch_shapes=[pltpu.VMEM((D_MODEL // LANES, tm, LANES), F32),
                        pltpu.VMEM((tm, D_FF), BF16),
                        pltpu.VMEM((FFN_CONV - 1, SUBLANES, 2 * D_FF), F32),
                        pltpu.VMEM((2 * (FFN_CONV - 1), SUBLANES + tm // ROW_CLASSES, FF_CHUNK), F32)],
        compiler_params=pltpu.CompilerParams(dimension_semantics=("arbitrary",),
                                             vmem_limit_bytes=VMEM_LIMIT),
        name="tail",
    )(x2d, attn, ssm, p2d, *consts)


def _q_perm():
    idx = np.arange(ATTN_DIM).reshape(ATTN_KV_HEADS, ATTN_GROUP, HEAD_DIM)
    return np.transpose(idx, (1, 0, 2)).reshape(-1)


def _layer(x, p, attn_norm_g, w_in, q_norm_g, k_norm_g, ssm_conv_w, ssm_conv_b, dt_bias, a_log,
           d_skip, ssm_norm_g, w_out, ffn_norm_g, w_up, ffn_conv_w, ffn_conv_b, w_down,
           ple_norm_g, w_ple_gate, w_ple_proj):
    bsz, seq, _ = x.shape
    t = bsz * seq
    x2d = x.reshape(t, D_MODEL)
    p2d = p.reshape(t, PLE_DIM)
    qperm = _q_perm()

    win = w_in.astype(BF16)
    wq = w_in[:, :ATTN_DIM][:, qperm].astype(BF16)
    head_reps = LANES // SSM_HEADS
    wdt = jnp.tile(w_in[:, COL_DT:], (1, head_reps)).astype(BF16)
    qkg = jnp.concatenate([jnp.tile(q_norm_g * (HEAD_DIM ** -0.5 * LOG2E), ATTN_Q_HEADS),
                           jnp.tile(k_norm_g, ATTN_KV_HEADS)]).reshape(1, QK_DIM)
    hid = np.arange(256) // HEAD_DIM
    bd = jnp.asarray(hid[:, None] == hid[None, :], BF16)
    dtb = jnp.tile(dt_bias, head_reps).reshape(1, LANES)
    alog = jnp.tile(a_log, head_reps).reshape(1, LANES)

    qkv, z, xbc, dt = _in_proj(x2d, attn_norm_g.reshape(1, D_MODEL), wq, win, wdt, qkg, bd,
                               ssm_conv_w, ssm_conv_b.reshape(1, XBC_DIM), dtb, seq)

    attn = _attention(qkv, jnp.asarray(_attn_bias()), bsz, seq)

    src_lane = np.arange(LANES)[:, None]
    expand = jnp.asarray((src_lane % SSM_HEADS == (np.arange(SSM_INNER) // SSM_HEAD_DIM)[None, :])
                         & (src_lane < 3 * SSM_HEADS), BF16)
    dskip = jnp.repeat(d_skip, SSM_HEAD_DIM).reshape(1, SSM_INNER)
    ssm = _ssd(xbc, dt, z, alog, expand, dskip, ssm_norm_g.reshape(1, SSM_INNER), bsz, seq)

    woa = w_out[:ATTN_DIM][qperm].astype(BF16)
    wos = w_out.astype(BF16)
    out = _tail(x2d, attn, ssm, p2d, woa, wos, ffn_norm_g.reshape(1, D_MODEL), w_up.astype(BF16),
                ffn_conv_w, ffn_conv_b.reshape(1, 2 * D_FF), w_down.astype(BF16),
                ple_norm_g.reshape(1, D_MODEL), w_ple_gate.astype(BF16), w_ple_proj.astype(BF16), seq)
    return out.reshape(bsz, seq, D_MODEL)


def kernel(x, p, attn_norm_g, w_in, q_norm_g, k_norm_g, ssm_conv_w, ssm_conv_b, dt_bias, a_log, d_skip,
           ssm_norm_g, w_out, ffn_norm_g, w_up, ffn_conv_w, ffn_conv_b, w_down, ple_norm_g, w_ple_gate,
           w_ple_proj):
    depth = w_in.shape[0]
    for i in range(depth):
        x = _layer(x, p[i], attn_norm_g[i], w_in[i], q_norm_g[i], k_norm_g[i], ssm_conv_w[i], ssm_conv_b[i],
                   dt_bias[i], a_log[i], d_skip[i], ssm_norm_g[i], w_out[i], ffn_norm_g[i], w_up[i],
                   ffn_conv_w[i], ffn_conv_b[i], w_down[i], ple_norm_g[i], w_ple_gate[i], w_ple_proj[i])
    return x
```

```python
import functools

import jax
import jax.numpy as jnp
import numpy as np
from jax import lax
from jax.experimental import pallas as pl
from jax.experimental.pallas import tpu as pltpu

F32 = jnp.float32
BF16 = jnp.bfloat16

D_MODEL = 1024
HEAD_DIM = 64
ATTN_Q_HEADS = 8
ATTN_KV_HEADS = 4
ATTN_GROUP = ATTN_Q_HEADS // ATTN_KV_HEADS
DILATED_PATTERNS = ((128, 1), (512, 4), (2048, 16))
ATTN_BLOCK = 128
ATTN_DIM = ATTN_Q_HEADS * HEAD_DIM
KV_DIM = ATTN_KV_HEADS * HEAD_DIM
SSM_HEADS = 16
SSM_HEAD_DIM = 64
SSM_INNER = SSM_HEADS * SSM_HEAD_DIM
SSM_GROUPS = 2
SSM_STATE = 128
SSM_CONV = 4
SSM_CHUNK = 128
BC_DIM = SSM_GROUPS * SSM_STATE
XBC_DIM = SSM_INNER + 2 * BC_DIM
D_FF = 2816
FFN_CONV = 3
PLE_DIM = 256
EPS = 1e-6
LOG2E = 1.4426950408889634

LANES = 128
SUBLANES = 8
QK_DIM = ATTN_DIM + KV_DIM
QKV_DIM = QK_DIM + KV_DIM
COL_K = ATTN_DIM
COL_V = COL_K + KV_DIM
COL_Z = COL_V + KV_DIM
COL_XBC = COL_Z + SSM_INNER
COL_DT = COL_XBC + XBC_DIM
QKV_SLABS = QKV_DIM // LANES
ATTN_SLABS = ATTN_DIM // LANES
KV_PAIRS = ATTN_KV_HEADS // 2
VMEM_LIMIT = 56 * 1024 * 1024

MAX_DIL = 16
SUPER = MAX_DIL * ATTN_BLOCK
IN_TM = 512
XBC_CHUNK = 512
SSD_STEP_CHUNKS = 8
TAIL_TM = 512
ROW_CLASSES = 8
FF_CHUNK = 256

assert DILATED_PATTERNS == ((128, 1), (512, 4), (2048, 16))
assert all(w // d == ATTN_BLOCK for w, d in DILATED_PATTERNS)


def _dot(a, b):
    return jnp.dot(a, b, preferred_element_type=F32)


def _dot_nt(a, b):
    return lax.dot_general(a, b, (((1,), (1,)), ((), ())), preferred_element_type=F32)


def _dot_tn(a, b):
    return lax.dot_general(a, b, (((0,), (0,)), ((), ())), preferred_element_type=F32)


def _rms(x):
    return x * lax.rsqrt(jnp.mean(x * x, axis=-1, keepdims=True) + EPS)


def _silu(x):
    return x * jax.nn.sigmoid(x)


def _const_spec(shape):
    nd = len(shape)
    return pl.BlockSpec(shape, lambda *_: (0,) * nd, pipeline_mode=pl.Buffered(1))


def _in_proj_kernel(tiles_per_batch,
                    x_ref, g_ref, wq_ref, win_ref, wdt_ref, qkg_ref, bd_ref,
                    cw_ref, cb_ref, dtb_ref,
                    qkv_ref, z_ref, xbc_ref, dt_ref, nat, mod4, *cbufs):
    i = pl.program_id(0)
    tm = x_ref.shape[0]

    @pl.when(i % tiles_per_batch == 0)
    def _():
        for cbuf in cbufs:
            cbuf[0:SUBLANES, :] = jnp.zeros((SUBLANES, XBC_CHUNK), F32)

    hb = (_rms(x_ref[...]) * g_ref[...]).astype(BF16)

    def ssm_conv(c):
        cbuf = cbufs[c]
        cs = slice(c * XBC_CHUNK, (c + 1) * XBC_CHUNK)
        cbuf[SUBLANES:SUBLANES + tm, :] = _dot(hb, win_ref[:, COL_XBC + c * XBC_CHUNK:COL_XBC + (c + 1) * XBC_CHUNK])
        acc = cb_ref[:, cs] + cw_ref[SSM_CONV - 1:SSM_CONV, cs] * cbuf[SUBLANES:SUBLANES + tm, :]
        for kk in range(SSM_CONV - 1):
            off = SUBLANES - (SSM_CONV - 1) + kk
            acc = acc + cw_ref[kk:kk + 1, cs] * cbuf[off:off + tm, :]
        xbc_ref[:, cs] = _silu(acc)
        cbuf[0:SUBLANES, :] = cbuf[tm:tm + SUBLANES, :]

    def qk_norm():
        qk = jnp.concatenate([_dot(hb, wq_ref[...]), _dot(hb, win_ref[:, COL_K:COL_V])], axis=1)
        sq = qk * qk
        sq_hi = sq.astype(BF16)
        sq_lo = (sq - sq_hi.astype(F32)).astype(BF16)
        bd = bd_ref[...]
        for c in range(QK_DIM // 256):
            sl = slice(c * 256, (c + 1) * 256)
            ss = _dot(sq_hi[:, sl], bd) + _dot(sq_lo[:, sl], bd)
            qkn = qk[:, sl] * lax.rsqrt(ss * (1.0 / HEAD_DIM) + EPS) * qkg_ref[:, sl]
            nat[2 * c] = qkn[:, :LANES]
            nat[2 * c + 1] = qkn[:, LANES:]

    def v_proj():
        v = _dot(hb, win_ref[:, COL_V:COL_Z])
        nat[QKV_SLABS - 2] = v[:, :LANES]
        nat[QKV_SLABS - 1] = v[:, LANES:]

    def relayout():
        quarter = tm // 4
        sixteenth = tm // MAX_DIL
        for s in range(QKV_SLABS):
            for a in range(4):
                mod4[s, a * quarter:(a + 1) * quarter, :] = nat[s, pl.ds(a, quarter, stride=4), :]
        for s in range(QKV_SLABS):
            for a in range(4):
                for b in range(4):
                    qkv_ref[s, 0, 4 * b + a, 0] = mod4[s, pl.ds(a * quarter + b, sixteenth, stride=4), :]

    def z_proj():
        z_ref[...] = _dot(hb, win_ref[:, COL_Z:COL_XBC]).astype(BF16)

    def dt_proj():
        dtr = _dot(hb, wdt_ref[...]) + dtb_ref[...]
        dt_ref[...] = jnp.maximum(dtr, 0.0) + jnp.log1p(jnp.exp(-jnp.abs(dtr)))

    for c in range(len(cbufs)):
        ssm_conv(c)
    qk_norm()
    v_proj()
    relayout()
    z_proj()
    dt_proj()


def _in_proj(x2d, g, wq, win, wdt, qkg, bd, cw, cb, dtb, seq):
    t = x2d.shape[0]
    tm = IN_TM
    assert SUPER % tm == 0 and tm % (4 * MAX_DIL) == 0 and seq % SUPER == 0
    grid = (t // tm,)
    row = lambda w: pl.BlockSpec((tm, w), lambda i: (i, 0))
    per_super = SUPER // tm
    qkv_shape = (QKV_SLABS, t // SUPER, MAX_DIL, per_super, tm // MAX_DIL, LANES)
    qkv, z, xbc, dt = pl.pallas_call(
        functools.partial(_in_proj_kernel, seq // tm),
        grid=grid,
        in_specs=[row(D_MODEL), _const_spec(g.shape), _const_spec(wq.shape), _const_spec(win.shape),
                  _const_spec(wdt.shape),
                  _const_spec(qkg.shape), _const_spec(bd.shape), _const_spec(cw.shape),
                  _const_spec(cb.shape), _const_spec(dtb.shape)],
        out_specs=[pl.BlockSpec((QKV_SLABS, 1, MAX_DIL, 1, tm // MAX_DIL, LANES),
                                lambda i: (0, i // per_super, 0, i % per_super, 0, 0)),
                   row(SSM_INNER), row(XBC_DIM), row(LANES)],
        out_shape=[jax.ShapeDtypeStruct(qkv_shape, F32), jax.ShapeDtypeStruct((t, SSM_INNER), BF16),
                   jax.ShapeDtypeStruct((t, XBC_DIM), F32), jax.ShapeDtypeStruct((t, LANES), F32)],
        scratch_shapes=[pltpu.VMEM((QKV_SLABS, tm, LANES), F32),
                        pltpu.VMEM((QKV_SLABS, tm, LANES), F32)]
                       + [pltpu.VMEM((tm + SUBLANES, XBC_CHUNK), F32)] * (XBC_DIM // XBC_CHUNK),
        compiler_params=pltpu.CompilerParams(dimension_semantics=("arbitrary",),
                                             vmem_limit_bytes=VMEM_LIMIT),
        name="in_proj",
    )(x2d, g, wq, win, wdt, qkg, bd, cw, cb, dtb)
    return qkv.reshape(QKV_SLABS, t, LANES), z, xbc, dt


def _block_perm(dil):
    reps = MAX_DIL // dil
    per = ATTN_BLOCK // reps
    pos = np.arange(ATTN_BLOCK)
    return (pos % per) * reps + pos // per


def _attn_bias():
    out = []
    for _, dil in DILATED_PATTERNS[::-1]:
        perm = _block_perm(dil)
        qi = perm[:, None]
        kj = np.concatenate([perm, ATTN_BLOCK + perm])[None, :]
        delta = ATTN_BLOCK + qi - kj
        band = (delta >= 0) & (delta <= ATTN_BLOCK)
        for first in (False, True):
            ok = band & (kj >= ATTN_BLOCK) if first else band
            out.append(np.tile(np.where(ok, 0.0, -np.inf).astype(np.float32), (4, 1)))
    return np.stack(out)


def _attn_kernel(cur_ref, prev_ref, bias_ref, out_ref, acc_o, acc_m, acc_l):
    first = (pl.program_id(1) == 0).astype(jnp.int32)
    lane = lax.broadcasted_iota(jnp.int32, (ATTN_BLOCK, LANES), 1)
    lo_half = lane < HEAD_DIM
    ones = jnp.ones((2 * ATTN_BLOCK, LANES), BF16)
    k_slab, v_slab = ATTN_SLABS, ATTN_SLABS + KV_PAIRS

    def rows(ref, slab, pieces):
        parts = [ref[slab, pl.ds(st, sz), :] for st, sz in pieces]
        return parts[0] if len(parts) == 1 else jnp.concatenate(parts, axis=0)

    def unit(q_pieces, prev_pieces, prev_in_cur, bias_idx, mode, out_start=None):
        bias = bias_ref[bias_idx]
        for p in range(KV_PAIRS):
            if prev_in_cur:
                kp, vp = rows(cur_ref, k_slab + p, prev_pieces), rows(cur_ref, v_slab + p, prev_pieces)
            else:
                kp, vp = rows(prev_ref, p, prev_pieces), rows(prev_ref, KV_PAIRS + p, prev_pieces)
            k2 = jnp.concatenate([kp, rows(cur_ref, k_slab + p, q_pieces)], axis=0).astype(BF16)
            v2 = jnp.concatenate([vp, rows(cur_ref, v_slab + p, q_pieces)], axis=0).astype(BF16)
            v2 = jnp.concatenate([v2, ones], axis=1)
            qs = []
            for g in range(ATTN_GROUP):
                q = rows(cur_ref, g * KV_PAIRS + p, q_pieces)
                qs += [jnp.where(lo_half, q, 0.0), jnp.where(lo_half, 0.0, q)]
            q4 = jnp.concatenate(qs, axis=0).astype(BF16)
            s = _dot_nt(q4, k2) + bias
            m = jnp.max(s, axis=-1, keepdims=True)
            pr = jnp.exp2(s - m).astype(BF16)
            ol = _dot(pr, v2)
            for g in range(ATTN_GROUP):
                r0, r1 = (2 * g) * ATTN_BLOCK, (2 * g + 1) * ATTN_BLOCK
                o = jnp.where(lo_half, ol[r0:r0 + ATTN_BLOCK, :LANES], ol[r1:r1 + ATTN_BLOCK, :LANES])
                l = jnp.where(lo_half, ol[r0:r0 + ATTN_BLOCK, LANES:], ol[r1:r1 + ATTN_BLOCK, LANES:])
                mm = jnp.where(lo_half, m[r0:r0 + ATTN_BLOCK], m[r1:r1 + ATTN_BLOCK])
                slab = g * KV_PAIRS + p
                if mode != "init":
                    m_old = rows(acc_m, slab, q_pieces)
                    m_new = jnp.maximum(m_old, mm)
                    w_old, w_new = jnp.exp2(m_old - m_new), jnp.exp2(mm - m_new)
                    o = rows(acc_o, slab, q_pieces) * w_old + o * w_new
                    l = rows(acc_l, slab, q_pieces) * w_old + l * w_new
                    mm = m_new
                if mode == "final":
                    res = o / l
                    for r in range(MAX_DIL):
                        out_ref[slab, pl.ds(out_start + r, SUBLANES, stride=MAX_DIL), :] = (
                            res[r * SUBLANES:(r + 1) * SUBLANES])
                else:
                    off = 0
                    for st, sz in q_pieces:
                        acc_o[slab, pl.ds(st, sz), :] = o[off:off + sz]
                        acc_m[slab, pl.ds(st, sz), :] = mm[off:off + sz]
                        acc_l[slab, pl.ds(st, sz), :] = l[off:off + sz]
                        off += sz

    def aligned(x, m):
        return x if isinstance(x, int) else pl.multiple_of(x, m)

    def d16_body(r, carry):
        st = pl.multiple_of(r * ATTN_BLOCK, ATTN_BLOCK)
        unit([(st, ATTN_BLOCK)], [(st, ATTN_BLOCK)], False, first, "init")
        return carry
    lax.fori_loop(0, MAX_DIL, d16_body, 0, unroll=MAX_DIL)

    def d4_pieces(c, row0):
        return [(aligned((4 * a + c) * ATTN_BLOCK + row0, 32), 32) for a in range(4)]

    def d4_body(c, carry):
        unit(d4_pieces(c, 0), d4_pieces(c, ATTN_BLOCK - 32), False, 2 + first, "merge")

        for n in range(1, 4):
            unit(d4_pieces(c, 32 * n), d4_pieces(c, 32 * (n - 1)), True, 2, "merge")
        return carry
    lax.fori_loop(0, 4, d4_body, 0, unroll=4)

    def d1_pieces(row0):
        return [(aligned(r * ATTN_BLOCK + row0, SUBLANES), SUBLANES) for r in range(MAX_DIL)]

    unit(d1_pieces(0), d1_pieces(ATTN_BLOCK - SUBLANES), False, 4 + first, "final", 0)

    def d1_body(n, carry):
        unit(d1_pieces(SUBLANES * n), d1_pieces(SUBLANES * (n - 1)), True, 4, "final", n * ATTN_BLOCK)
        return carry
    lax.fori_loop(1, SUPER // ATTN_BLOCK, d1_body, 0, unroll=SUPER // ATTN_BLOCK - 1)


def _attention(qkv, bias, bsz, seq):
    t = bsz * seq
    nsb = seq // SUPER
    acc = pltpu.VMEM((ATTN_SLABS, SUPER, LANES), F32)
    return pl.pallas_call(
        _attn_kernel,
        grid=(bsz, nsb),
        in_specs=[pl.BlockSpec((QKV_SLABS, SUPER, LANES), lambda b, s: (0, b * nsb + s, 0)),
                  pl.BlockSpec((QKV_SLABS // 2, SUPER, LANES),
                               lambda b, s: (1, b * nsb + jnp.maximum(s - 1, 0), 0)),
                  _const_spec(bias.shape)],
        out_specs=pl.BlockSpec((ATTN_SLABS, SUPER, LANES), lambda b, s: (0, b * nsb + s, 0)),
        out_shape=jax.ShapeDtypeStruct((ATTN_SLABS, t, LANES), F32),
        scratch_shapes=[acc, acc, acc],
        compiler_params=pltpu.CompilerParams(dimension_semantics=("arbitrary", "arbitrary"),
                                             vmem_limit_bytes=VMEM_LIMIT),
        name="attention",
    )(qkv, qkv, bias)


def _ssd_kernel(xbc_ref, dt_ref, dtn_ref, z_ref, alog_ref, expand_ref, dskip_ref, ng_ref, out_ref,
                state_ref, y_ref, xde_ref, e_ref, *decay_refs):
    nsub = SSD_STEP_CHUNKS
    cur = [decay_refs[3 * k:3 * k + 3] for k in range(nsub)]
    nxt = [decay_refs[3 * (nsub + k):3 * (nsub + k) + 3] for k in range(nsub)]
    c = pl.program_id(1)
    ln = SSM_CHUNK

    lane = lax.broadcasted_iota(jnp.int32, (ln, LANES), 1)
    lo_half = lane < SSM_HEAD_DIM
    part = lane // SSM_HEADS
    row = lax.broadcasted_iota(jnp.int32, (ln, ln), 0)
    col = lax.broadcasted_iota(jnp.int32, (ln, ln), 1)
    causal = row >= col
    tril = causal.astype(BF16)

    def split3(v):
        hi = v.astype(BF16).astype(F32)
        mid = (v - hi).astype(BF16).astype(F32)
        lo = (v - hi) - mid
        return hi, mid, lo

    def pack3(v):
        hi, mid, lo = split3(v)
        return jnp.where(part == 0, hi, jnp.where(part == 1, mid, jnp.where(part == 2, lo, 0.0))).astype(BF16)

    def decay_chain(dt, dst):
        a = dt * (-LOG2E * jnp.exp(alog_ref[...]))
        a3 = jnp.concatenate(split3(a), axis=0).astype(BF16)
        acum = _dot(jnp.concatenate([tril, tril, tril], axis=1), a3)
        total = acum[ln - 1:ln, :]
        packed = jnp.concatenate([pack3(dt), pack3(jnp.exp2(acum)), pack3(jnp.exp2(total - acum))], axis=0)
        packed_ref, acum_ref, acumt_ref = dst
        packed_ref[...] = packed
        acum_ref[...] = acum
        acumt_ref[...] = acum.T

    @pl.when(c == 0)
    def _():
        state_ref[...] = jnp.zeros_like(state_ref)
        for k in range(nsub):
            decay_chain(dt_ref[k * ln:(k + 1) * ln, :], cur[k])

    for k in range(nsub):
        decay_chain(dtn_ref[k * ln:(k + 1) * ln, :], nxt[k])

    gw = SSM_INNER // SSM_GROUPS
    pairs = SSM_HEADS // SSM_GROUPS // 2
    for k in range(nsub):
        rows = slice(k * ln, (k + 1) * ln)
        e = e_ref.at[k]
        xde = xde_ref.at[k]
        bgs, cbs, y_offs = [], [], []
        for g in range(SSM_GROUPS):
            bg = xbc_ref[rows, SSM_INNER + g * SSM_STATE:SSM_INNER + (g + 1) * SSM_STATE].astype(BF16)
            cg = xbc_ref[rows, SSM_INNER + BC_DIM + g * SSM_STATE:
                         SSM_INNER + BC_DIM + (g + 1) * SSM_STATE].astype(BF16)
            bgs.append(bg)
            cbs.append(_dot_nt(cg, bg))
            y_offs.append(_dot(cg, state_ref[g].astype(BF16)))

        e[...] = _dot(cur[k][0][...], expand_ref[...])
        acum = cur[k][1][...]
        acum_t = cur[k][2][...]
        ssq = jnp.zeros((ln, LANES), F32)
        for g in range(SSM_GROUPS):
            for j in range(pairs):
                h0 = g * 2 * pairs + 2 * j
                ps = slice(h0 * SSM_HEAD_DIM, (h0 + 2) * SSM_HEAD_DIM)
                xs = xbc_ref[rows, ps]
                xdt = xs * e[0:ln, ps]
                ms = []
                for h in (h0, h0 + 1):
                    seg = acum[:, h:h + 1] - acum_t[h:h + 1, :]
                    ms.append(cbs[g] * jnp.exp2(jnp.where(causal, seg, -jnp.inf)))
                mm = jnp.concatenate(ms, axis=1).astype(BF16)
                rhs = jnp.concatenate([jnp.where(lo_half, xdt, 0.0), jnp.where(lo_half, 0.0, xdt)],
                                      axis=0).astype(BF16)
                y = (_dot(mm, rhs) + y_offs[g][:, j * LANES:(j + 1) * LANES] * e[ln:2 * ln, ps]
                     + dskip_ref[:, ps] * xs)
                y = y * _silu(z_ref[rows, ps].astype(F32))
                y_ref[rows, ps] = y
                ssq = ssq + y * y
                xde[:, ps] = (xdt * e[2 * ln:3 * ln, ps]).astype(BF16)
            gs = slice(g * gw, (g + 1) * gw)
            state_ref[g] = state_ref[g] * e[2 * ln - 1:2 * ln, gs] + _dot_tn(bgs[g], xde[:, gs])

        scale = lax.rsqrt(jnp.sum(ssq, axis=-1, keepdims=True) * (1.0 / SSM_INNER) + EPS)
        out_ref[rows, :] = (y_ref[rows, :] * scale * ng_ref[...]).astype(BF16)

    for k in range(nsub):
        for dst, src in zip(cur[k], nxt[k]):
            dst[...] = src[...]


def _ssd(xbc, dt, z, alog, expand, dskip, ng, bsz, seq):
    t = bsz * seq
    nsub = SSD_STEP_CHUNKS
    rows = nsub * SSM_CHUNK
    assert seq % rows == 0
    ns = seq // rows
    row = lambda w: pl.BlockSpec((rows, w), lambda b, c: (b * ns + c, 0))
    nxt = pl.BlockSpec((rows, LANES), lambda b, c: (b * ns + jnp.minimum(c + 1, ns - 1), 0))
    decay_scratch = [pltpu.VMEM((3 * SSM_CHUNK, LANES), BF16), pltpu.VMEM((SSM_CHUNK, LANES), F32),
                     pltpu.VMEM((LANES, SSM_CHUNK), F32)]
    return pl.pallas_call(
        _ssd_kernel,
        grid=(bsz, ns),
        in_specs=[row(XBC_DIM), row(LANES), nxt, row(SSM_INNER), _const_spec(alog.shape),
                  _const_spec(expand.shape), _const_spec(dskip.shape), _const_spec(ng.shape)],
        out_specs=row(SSM_INNER),
        out_shape=jax.ShapeDtypeStruct((t, SSM_INNER), BF16),
        scratch_shapes=[pltpu.VMEM((SSM_GROUPS, SSM_STATE, SSM_INNER // SSM_GROUPS), F32),
                        pltpu.VMEM((rows, SSM_INNER), F32),
                        pltpu.VMEM((nsub, SSM_CHUNK, SSM_INNER), BF16),
                        pltpu.VMEM((nsub, 3 * SSM_CHUNK, SSM_INNER), F32)] + decay_scratch * (2 * nsub),
        compiler_params=pltpu.CompilerParams(dimension_semantics=("arbitrary", "arbitrary"),
                                             vmem_limit_bytes=VMEM_LIMIT),
        name="ssd",
    )(xbc, dt, dt, z, alog, expand, dskip, ng)


def _tail_kernel(tiles_per_batch,
                 x_ref, attn_ref, ssm_ref, p_ref, woa_ref, wos_ref, fg_ref, wup_ref, fcw_ref, fcb_ref, wdn_ref,
                 pg_ref, wpg_ref, wpp_ref, out_ref, slabs, act_ref, carry, sbuf):
    i = pl.program_id(0)
    tm = x_ref.shape[0]
    n8 = tm // ROW_CLASSES
    nslab = D_MODEL // LANES

    @pl.when(i % tiles_per_batch == 0)
    def _():
        carry[...] = jnp.zeros_like(carry)

    attn = jnp.concatenate([attn_ref[s] for s in range(ATTN_SLABS)], axis=1).astype(BF16)
    x1 = x_ref[...] + _dot(ssm_ref[...], wos_ref[ATTN_DIM:, :]) + _dot(attn, woa_ref[...])

    pp = _dot(p_ref[...].astype(BF16), wpp_ref[...])

    h2 = _rms(x1) * fg_ref[...]
    for s in range(nslab):
        slabs[s] = h2[:, s * LANES:(s + 1) * LANES]
    h2p = jnp.concatenate(
        [jnp.concatenate([slabs[s, pl.ds(q, n8, stride=ROW_CLASSES), :] for q in range(ROW_CLASSES)], axis=0)
         for s in range(nslab)], axis=1).astype(BF16)

    def conv(u, cs, slot):
        blk = [u[q * n8:(q + 1) * n8] for q in range(ROW_CLASSES)]
        shifted = []
        for k in range(FFN_CONV - 1):
            sb = sbuf.at[slot * (FFN_CONV - 1) + k]
            sb[0:SUBLANES, :] = carry[k, :, cs]
            sb[SUBLANES:SUBLANES + n8, :] = blk[ROW_CLASSES - (FFN_CONV - 1) + k]
            shifted.append(sb[SUBLANES - 1:SUBLANES - 1 + n8, :])
            carry[k, :, cs] = sb[n8:n8 + SUBLANES, :]
        prev = shifted + blk
        w = [fcw_ref[k:k + 1, cs] for k in range(FFN_CONV)]
        b = fcb_ref[:, cs]
        return jnp.concatenate([b + w[2] * prev[q + 2] + w[1] * prev[q + 1] + w[0] * prev[q]
                                for q in range(ROW_CLASSES)], axis=0)

    for c in range(D_FF // FF_CHUNK):
        gs = slice(c * FF_CHUNK, (c + 1) * FF_CHUNK)
        vs = slice(D_FF + c * FF_CHUNK, D_FF + (c + 1) * FF_CHUNK)
        gate = conv(_dot(h2p, wup_ref[:, gs]), gs, 0)
        val = conv(_dot(h2p, wup_ref[:, vs]), vs, 1)
        act_ref[:, gs] = (_silu(gate) * val).astype(BF16)
    ffn_p = _dot(act_ref[...], wdn_ref[...])

    for s in range(nslab):
        for q in range(ROW_CLASSES):
            slabs[s, pl.ds(q, n8, stride=ROW_CLASSES), :] = ffn_p[q * n8:(q + 1) * n8, s * LANES:(s + 1) * LANES]
    x2 = x1 + jnp.concatenate([slabs[s] for s in range(nslab)], axis=1)

    h3 = (_rms(x2) * pg_ref[...]).astype(BF16)
    gate = jax.nn.sigmoid(_dot(h3, wpg_ref[...]))
    out_ref[...] = x2 + gate * pp


def _tail(x2d, attn, ssm, p2d, woa, wos, fg, wup, fcw, fcb, wdn, pg, wpg, wpp, seq):
    t = x2d.shape[0]
    tm = TAIL_TM
    assert seq % tm == 0 and tm % (ROW_CLASSES * SUBLANES) == 0 and D_FF % FF_CHUNK == 0
    row = lambda w: pl.BlockSpec((tm, w), lambda i: (i, 0))
    consts = [woa, wos, fg, wup, fcw, fcb, wdn, pg, wpg, wpp]
    return pl.pallas_call(
        functools.partial(_tail_kernel, seq // tm),
        grid=(t // tm,),
        in_specs=[row(D_MODEL), pl.BlockSpec((ATTN_SLABS, tm, LANES), lambda i: (0, i, 0)),
                  row(SSM_INNER), row(PLE_DIM)] + [_const_spec(w.shape) for w in consts],
        out_specs=row(D_MODEL),
        out_shape=jax.ShapeDtypeStruct((t, D_MODEL), F32),
        scratch_shapes=[pltpu.VMEM((D_MODEL // LANES, tm, LANES), F32),
                        pltpu.VMEM((tm, D_FF), BF16),
                        pltpu.VMEM((FFN_CONV - 1, SUBLANES, 2 * D_FF), F32),
                        pltpu.VMEM((2 * (FFN_CONV - 1), SUBLANES + tm // ROW_CLASSES, FF_CHUNK), F32)],
        compiler_params=pltpu.CompilerParams(dimension_semantics=("arbitrary",),
                                             vmem_limit_bytes=VMEM_LIMIT),
        name="tail",
    )(x2d, attn, ssm, p2d, *consts)


def _q_perm():
    idx = np.arange(ATTN_DIM).reshape(ATTN_KV_HEADS, ATTN_GROUP, HEAD_DIM)
    return np.transpose(idx, (1, 0, 2)).reshape(-1)


def _layer(x, p, attn_norm_g, w_in, q_norm_g, k_norm_g, ssm_conv_w, ssm_conv_b, dt_bias, a_log,
           d_skip, ssm_norm_g, w_out, ffn_norm_g, w_up, ffn_conv_w, ffn_conv_b, w_down,
           ple_norm_g, w_ple_gate, w_ple_proj):
    bsz, seq, _ = x.shape
    t = bsz * seq
    x2d = x.reshape(t, D_MODEL)
    p2d = p.reshape(t, PLE_DIM)
    qperm = _q_perm()

    win = w_in.astype(BF16)
    wq = w_in[:, :ATTN_DIM][:, qperm].astype(BF16)
    head_reps = LANES // SSM_HEADS
    wdt = jnp.tile(w_in[:, COL_DT:], (1, head_reps)).astype(BF16)
    qkg = jnp.concatenate([jnp.tile(q_norm_g * (HEAD_DIM ** -0.5 * LOG2E), ATTN_Q_HEADS),
                           jnp.tile(k_norm_g, ATTN_KV_HEADS)]).reshape(1, QK_DIM)
    hid = np.arange(256) // HEAD_DIM
    bd = jnp.asarray(hid[:, None] == hid[None, :], BF16)
    dtb = jnp.tile(dt_bias, head_reps).reshape(1, LANES)
    alog = jnp.tile(a_log, head_reps).reshape(1, LANES)

    qkv, z, xbc, dt = _in_proj(x2d, attn_norm_g.reshape(1, D_MODEL), wq, win, wdt, qkg, bd,
                               ssm_conv_w, ssm_conv_b.reshape(1, XBC_DIM), dtb, seq)

    attn = _attention(qkv, jnp.asarray(_attn_bias()), bsz, seq)

    src_lane = np.arange(LANES)[:, None]
    expand = jnp.asarray((src_lane % SSM_HEADS == (np.arange(SSM_INNER) // SSM_HEAD_DIM)[None, :])
                         & (src_lane < 3 * SSM_HEADS), BF16)
    dskip = jnp.repeat(d_skip, SSM_HEAD_DIM).reshape(1, SSM_INNER)
    ssm = _ssd(xbc, dt, z, alog, expand, dskip, ssm_norm_g.reshape(1, SSM_INNER), bsz, seq)

    woa = w_out[:ATTN_DIM][qperm].astype(BF16)
    wos = w_out.astype(BF16)
    out = _tail(x2d, attn, ssm, p2d, woa, wos, ffn_norm_g.reshape(1, D_MODEL), w_up.astype(BF16),
                ffn_conv_w, ffn_conv_b.reshape(1, 2 * D_FF), w_down.astype(BF16),
                ple_norm_g.reshape(1, D_MODEL), w_ple_gate.astype(BF16), w_ple_proj.astype(BF16), seq)
    return out.reshape(bsz, seq, D_MODEL)


def kernel(x, p, attn_norm_g, w_in, q_norm_g, k_norm_g, ssm_conv_w, ssm_conv_b, dt_bias, a_log, d_skip,
           ssm_norm_g, w_out, ffn_norm_g, w_up, ffn_conv_w, ffn_conv_b, w_down, ple_norm_g, w_ple_gate,
           w_ple_proj):
    depth = w_in.shape[0]
    for i in range(depth):
        x = _layer(x, p[i], attn_norm_g[i], w_in[i], q_norm_g[i], k_norm_g[i], ssm_conv_w[i], ssm_conv_b[i],
                   dt_bias[i], a_log[i], d_skip[i], ssm_norm_g[i], w_out[i], ffn_norm_g[i], w_up[i],
                   ffn_conv_w[i], ffn_conv_b[i], w_down[i], ple_norm_g[i], w_ple_gate[i], w_ple_proj[i])
    return x
```
